```python
import jax, jax.numpy as jnp
from jax import lax
import numpy as np


D_MODEL = 1024
BATCH = 1
SEQ = 16384
DEPTH = 2
DEC_BATCH = 32
DEC_SEQ = 2048
PAST_LEN = 128

N_EVEN = (DEPTH + 1) // 2
N_ODD = DEPTH // 2

HG_HEADS = 4
HG_DK = 128
HG_DV = 128
HG_WIDTH = HG_HEADS * HG_DK
HG_CHUNK = 64

MLA_HEADS = 4
MLA_Q_RANK = 384
MLA_KV_RANK = 256
MLA_NOPE = 128
MLA_ROPE = 64
MLA_V = 128
MLA_QK = MLA_NOPE + MLA_ROPE
ROPE_THETA = 10000.0
Q_BLOCK = 128

EVEN_IN = 5 * HG_WIDTH + MLA_Q_RANK + MLA_KV_RANK + MLA_ROPE
EVEN_MIX = HG_HEADS * HG_DV + MLA_HEADS * MLA_V

LRU_WIDTH = D_MODEL
LRU_BLOCKS = 4
LRU_BW = LRU_WIDTH // LRU_BLOCKS
CONV_W = 4
CONV_LEFT = 2
LRU_C = 8.0

N_EXPERTS = 16
EXPERT_FF = 1024
CAPACITY = 2

EPS = 1e-6

kernel_name = 'hybrid_hgrn2_mla_rglru_ecmoe_encoder'


def rmsnorm(x, g):
    xf = x.astype(jnp.float32)
    y = xf * lax.rsqrt(jnp.mean(xf * xf, axis=-1, keepdims=True) + EPS)
    return (y * g.astype(jnp.float32)).astype(x.dtype)


def _flip(t):
    return jnp.flip(t, axis=1)


def _gla_direction(q, k, v, logf):
    B, S, H, DK = q.shape
    DV = v.shape[-1]
    n = S // HG_CHUNK

    def to_chunks(t):
        return t.astype(jnp.float32).reshape(B, n, HG_CHUNK, H, t.shape[-1]).transpose(1, 0, 3, 2, 4)

    qc, kc, vc, fc = (to_chunks(t) for t in (q, k, v, logf))
    lower = jnp.tril(jnp.ones((HG_CHUNK, HG_CHUNK), dtype=bool))[:, :, None]

    def step(state, inp):
        qi, ki, vi, fi = inp
        b = jnp.cumsum(fi, axis=-2)
        b_last = b[..., -1:, :]
        diff = b[..., :, None, :] - b[..., None, :, :]
        decay = jnp.exp(jnp.where(lower, diff, -jnp.inf))
        scores = jnp.einsum('bhtd,bhsd,bhtsd->bhts', qi, ki, decay)
        o = (jnp.einsum('bhts,bhse->bhte', scores, vi)
             + jnp.einsum('bhtd,bhde->bhte', qi * jnp.exp(b), state))
        state = (state * jnp.exp(b_last)[..., 0, :, None]
                 + jnp.einsum('bhsd,bhse->bhde', ki * jnp.exp(b_last - b), vi))
        return state, o

    s0 = jnp.zeros((B, H, DK, DV), jnp.float32)
    _, o = lax.scan(step, s0, (qc, kc, vc, fc))
    return o.transpose(1, 0, 3, 2, 4).reshape(B, S, H, DV)


def hgrn2_mixer(q_in, i_in, g_in, f_fwd, f_bwd, lb, out_gain):
    B, S, _ = q_in.shape
    shp = (B, S, HG_HEADS, HG_DK)
    q = q_in.reshape(shp) * (HG_DK ** -0.5)
    v = i_in.reshape(B, S, HG_HEADS, HG_DV)
    o = jnp.zeros((B, S, HG_HEADS, HG_DV), jnp.float32)
    for d, fz in enumerate((f_fwd, f_bwd)):
        f = (lb[d] + (1.0 - lb[d]) * jax.nn.sigmoid(fz.astype(jnp.float32))).reshape(shp)
        logf = jnp.log(f)
        k = 1.0 - f
        if d == 0:
            o = o + _gla_direction(q, k, v, logf)
        else:
            o = o + _flip(_gla_direction(_flip(q), _flip(k), _flip(v), _flip(logf)))
    gate = jax.nn.silu(g_in.astype(jnp.float32)).reshape(B, S, HG_HEADS, HG_DV)
    o = rmsnorm(o, out_gain) * gate
    return o.reshape(B, S, HG_HEADS * HG_DV).astype(q_in.dtype)


def rope_tables(S):
    pos = jnp.arange(S, dtype=jnp.float32)
    inv = 1.0 / (ROPE_THETA ** (jnp.arange(0, MLA_ROPE, 2, dtype=jnp.float32) / MLA_ROPE))
    ang = pos[:, None] * inv[None, :]
    return jnp.cos(ang), jnp.sin(ang)


def apply_rope(x, cos, sin):
    half = MLA_ROPE // 2
    x1, x2 = x[..., :half], x[..., half:]
    c = cos[None, :, None, :]
    s = sin[None, :, None, :]
    return jnp.concatenate([x1 * c - x2 * s, x1 * s + x2 * c], axis=-1).astype(x.dtype)


def bidir_block_attention(q, k, v):
    B, S, H, Dh = q.shape
    Dv = v.shape[-1]
    nb = S // Q_BLOCK
    qb = q.reshape(B, nb, Q_BLOCK, H, Dh).transpose(1, 0, 2, 3, 4)
    scale = Dh ** -0.5

    def one_block(qblk):
        s = jnp.einsum('bqhd,bkhd->bhqk', qblk, k, preferred_element_type=jnp.float32) * scale
        p = jax.nn.softmax(s, axis=-1)
        return jnp.einsum('bhqk,bkhe->bqhe', p.astype(v.dtype), v)

    o = lax.map(one_block, qb)
    return o.transpose(1, 0, 2, 3, 4).reshape(B, S, H, Dv)


def mla_mixer(c_q, c_kv, k_pe, q_norm_g, kv_norm_g, w_uq, w_ukv, q_gain, k_gain):
    B, S, _ = c_q.shape
    q = jnp.einsum('bsr,rn->bsn', rmsnorm(c_q, q_norm_g), w_uq).reshape(B, S, MLA_HEADS, MLA_QK)
    kv = jnp.einsum('bsr,rn->bsn', rmsnorm(c_kv, kv_norm_g), w_ukv).reshape(B, S, MLA_HEADS, MLA_NOPE + MLA_V)
    k_nope, v = kv[..., :MLA_NOPE], kv[..., MLA_NOPE:]
    k = jnp.concatenate([k_nope, jnp.broadcast_to(k_pe[:, :, None, :], (B, S, MLA_HEADS, MLA_ROPE))], axis=-1)
    q = rmsnorm(q, q_gain)
    k = rmsnorm(k, k_gain)
    cos, sin = rope_tables(S)
    q = jnp.concatenate([q[..., :MLA_NOPE], apply_rope(q[..., MLA_NOPE:], cos, sin)], axis=-1)
    k = jnp.concatenate([k[..., :MLA_NOPE], apply_rope(k[..., MLA_NOPE:], cos, sin)], axis=-1)
    o = bidir_block_attention(q, k, v)
    return o.reshape(B, S, MLA_HEADS * MLA_V)


def even_mixer(h, w_in, w_out, lb, hg_gain, q_norm_g, kv_norm_g, w_uq, w_ukv, q_gain, k_gain):
    proj = jnp.einsum('bsd,dn->bsn', h, w_in)
    sizes = [HG_WIDTH] * 5 + [MLA_Q_RANK, MLA_KV_RANK, MLA_ROPE]
    cuts = [int(c) for c in np.cumsum(sizes)[:-1]]
    q_in, i_in, g_in, f_f, f_b, c_q, c_kv, k_pe = jnp.split(proj, cuts, axis=-1)
    o_hg = hgrn2_mixer(q_in, i_in, g_in, f_f, f_b, lb, hg_gain)
    o_mla = mla_mixer(c_q, c_kv, k_pe, q_norm_g, kv_norm_g, w_uq, w_ukv, q_gain, k_gain)
    return jnp.einsum('bsn,nd->bsd', jnp.concatenate([o_hg, o_mla.astype(o_hg.dtype)], axis=-1), w_out)


def centred_dwconv(x, w, b):
    S = x.shape[1]
    xp = jnp.pad(x, ((0, 0), (CONV_LEFT, CONV_W - 1 - CONV_LEFT), (0, 0)))
    out = xp[:, 0:S] * w[0]
    for j in range(1, CONV_W):
        out = out + xp[:, j:j + S] * w[j]
    return out + b


def block_diag_linear(x, w, b):
    xb = x.reshape(x.shape[0], x.shape[1], LRU_BLOCKS, LRU_BW)
    return jnp.einsum('bsnk,nkj->bsnj', xb, w).reshape(x.shape) + b


def rglru_direction(x, wa, ba, wx, bx, lam, reverse):
    r = jax.nn.sigmoid(block_diag_linear(x, wa, ba).astype(jnp.float32))
    i = jax.nn.sigmoid(block_diag_linear(x, wx, bx).astype(jnp.float32))
    log_a = -LRU_C * r * jax.nn.softplus(-lam.astype(jnp.float32))
    a = jnp.exp(log_a)
    u = jnp.sqrt(-jnp.expm1(2.0 * log_a)) * (i * x.astype(jnp.float32))

    def step(hc, au):
        a_t, u_t = au
        hc = a_t * hc + u_t
        return hc, hc

    h0 = jnp.zeros((x.shape[0], x.shape[2]), jnp.float32)
    _, hs = lax.scan(step, h0, (a.transpose(1, 0, 2), u.transpose(1, 0, 2)), reverse=reverse)
    return hs.transpose(1, 0, 2)


def odd_mixer(h, w_in, conv_w, conv_b, wa, ba, wx, bx, lam, w_out):
    proj = jnp.einsum('bsd,dn->bsn', h, w_in)
    gate, xb = proj[..., :LRU_WIDTH], proj[..., LRU_WIDTH:]
    xb = centred_dwconv(xb, conv_w, conv_b)
    y = (rglru_direction(xb, wa[0], ba[0], wx[0], bx[0], lam[0], False)
         + rglru_direction(xb, wa[1], ba[1], wx[1], bx[1], lam[1], True))
    y = (jax.nn.gelu(gate.astype(jnp.float32)) * y).astype(h.dtype)
    return jnp.einsum('bsn,nd->bsd', y, w_out)


def expert_choice_moe(h, w_router, w_gate, w_up, w_down):
    B, S, D = h.shape
    xt = h.reshape(B * S, D)
    cap = CAPACITY * (B * S) // N_EXPERTS
    affinity = jax.nn.softmax(jnp.einsum('nd,de->ne', xt, w_router, preferred_element_type=jnp.float32), axis=-1)
    gates, idx = lax.top_k(affinity.T, cap)
    xe = xt[idx]
    hid = jax.nn.silu(jnp.einsum('ecd,edf->ecf', xe, w_gate)) * jnp.einsum('ecd,edf->ecf', xe, w_up)
    ye = jnp.einsum('ecf,efd->ecd', hid, w_down) * gates[..., None].astype(xt.dtype)
    out = jnp.zeros_like(xt).at[idx.reshape(-1)].add(ye.reshape(-1, D))
    return out.reshape(B, S, D)


def trunk(x, norm_mix, norm_ffn, ev_w_in, ev_w_out, hg_lb_logits, hg_out_gain, mla_q_norm, mla_kv_norm,
          mla_w_uq, mla_w_ukv, mla_q_gain, mla_k_gain, od_w_in, od_conv_w, od_conv_b, rg_w_a, rg_b_a,
          rg_w_x, rg_b_x, rg_lambda, od_w_out, moe_router, moe_w_gate, moe_w_up, moe_w_down):
    lb_all = jnp.cumsum(jax.nn.softmax(hg_lb_logits.astype(jnp.float32), axis=1), axis=1)
    for layer in range(DEPTH):
        j = layer // 2
        h = rmsnorm(x, norm_mix[layer])
        if layer % 2 == 0:
            x = x + even_mixer(h, ev_w_in[j], ev_w_out[j], lb_all[:, j], hg_out_gain[j], mla_q_norm[j],
                               mla_kv_norm[j], mla_w_uq[j], mla_w_ukv[j], mla_q_gain[j], mla_k_gain[j])
        else:
            x = x + odd_mixer(h, od_w_in[j], od_conv_w[j], od_conv_b[j], rg_w_a[j], rg_b_a[j],
                              rg_w_x[j], rg_b_x[j], rg_lambda[j], od_w_out[j])
        x = x + expert_choice_moe(rmsnorm(x, norm_ffn[layer]), moe_router[layer], moe_w_gate[layer],
                                  moe_w_up[layer], moe_w_down[layer])
    return x


def setup_inputs(seed: int = 0) -> dict:
    key = jax.random.key(seed)
    ks = jax.random.split(key, 32)
    f32 = jnp.float32

    def nrm(k, shape, fan_in):
        return jax.random.normal(k, shape, f32) * (fan_in ** -0.5)

    def gain(k, shape):
        return 1.0 + 0.02 * jax.random.normal(k, shape, f32)

    def small(k, shape):
        return 0.01 * jax.random.normal(k, shape, f32)

    a0 = jax.random.uniform(ks[21], (N_ODD, 2, LRU_WIDTH), f32, minval=0.9, maxval=0.999)
    s = a0 ** (1.0 / LRU_C)
    rg_lambda = jnp.log(s) - jnp.log1p(-s)
    return {
        'x_prompt': jax.random.normal(ks[0], (BATCH, SEQ, D_MODEL), f32),
        'x_sample': jax.random.normal(ks[1], (DEC_BATCH, DEC_SEQ, D_MODEL), f32),
        'norm_mix': gain(ks[2], (DEPTH, D_MODEL)),
        'norm_ffn': gain(ks[3], (DEPTH, D_MODEL)),
        'ev_w_in': nrm(ks[4], (N_EVEN, D_MODEL, EVEN_IN), D_MODEL),
        'ev_w_out': nrm(ks[5], (N_EVEN, EVEN_MIX, D_MODEL), EVEN_MIX),
        'hg_lb_logits': 0.5 * jax.random.normal(ks[6], (2, N_EVEN + 1, HG_WIDTH), f32),
        'hg_out_gain': gain(ks[7], (N_EVEN, HG_DV)),
        'mla_q_norm': gain(ks[8], (N_EVEN, MLA_Q_RANK)),
        'mla_kv_norm': gain(ks[9], (N_EVEN, MLA_KV_RANK)),
        'mla_w_uq': nrm(ks[10], (N_EVEN, MLA_Q_RANK, MLA_HEADS * MLA_QK), MLA_Q_RANK),
        'mla_w_ukv': nrm(ks[11], (N_EVEN, MLA_KV_RANK, MLA_HEADS * (MLA_NOPE + MLA_V)), MLA_KV_RANK),
        'mla_q_gain': gain(ks[12], (N_EVEN, MLA_QK)),
        'mla_k_gain': gain(ks[13], (N_EVEN, MLA_QK)),
        'od_w_in': nrm(ks[14], (N_ODD, D_MODEL, 2 * LRU_WIDTH), D_MODEL),
        'od_conv_w': nrm(ks[15], (N_ODD, CONV_W, LRU_WIDTH), CONV_W),
        'od_conv_b': small(ks[16], (N_ODD, LRU_WIDTH)),
        'rg_w_a': nrm(ks[17], (N_ODD, 2, LRU_BLOCKS, LRU_BW, LRU_BW), LRU_BW),
        'rg_b_a': small(ks[18], (N_ODD, 2, LRU_WIDTH)),
        'rg_w_x': nrm(ks[19], (N_ODD, 2, LRU_BLOCKS, LRU_BW, LRU_BW), LRU_BW),
        'rg_b_x': small(ks[20], (N_ODD, 2, LRU_WIDTH)),
        'rg_lambda': rg_lambda,
        'od_w_out': nrm(ks[22], (N_ODD, LRU_WIDTH, D_MODEL), LRU_WIDTH),
        'moe_router': nrm(ks[23], (DEPTH, D_MODEL, N_EXPERTS), D_MODEL),
        'moe_w_gate': nrm(ks[24], (DEPTH, N_EXPERTS, D_MODEL, EXPERT_FF), D_MODEL),
        'moe_w_up': nrm(ks[25], (DEPTH, N_EXPERTS, D_MODEL, EXPERT_FF), D_MODEL),
        'moe_w_down': nrm(ks[26], (DEPTH, N_EXPERTS, EXPERT_FF, D_MODEL), EXPERT_FF),
    }


def reference(x_prompt, x_sample, norm_mix, norm_ffn, ev_w_in, ev_w_out, hg_lb_logits, hg_out_gain,
              mla_q_norm, mla_kv_norm, mla_w_uq, mla_w_ukv, mla_q_gain, mla_k_gain, od_w_in, od_conv_w,
              od_conv_b, rg_w_a, rg_b_a, rg_w_x, rg_b_x, rg_lambda, od_w_out, moe_router, moe_w_gate,
              moe_w_up, moe_w_down):
    params = (norm_mix, norm_ffn, ev_w_in, ev_w_out, hg_lb_logits, hg_out_gain, mla_q_norm, mla_kv_norm,
              mla_w_uq, mla_w_ukv, mla_q_gain, mla_k_gain, od_w_in, od_conv_w, od_conv_b, rg_w_a, rg_b_a,
              rg_w_x, rg_b_x, rg_lambda, od_w_out, moe_router, moe_w_gate, moe_w_up, moe_w_down)
    y_prompt = trunk(x_prompt, *params)
    y_sample = trunk(x_sample, *params)
    return (y_prompt, y_sample)
```

```python
import functools

import jax
import jax.numpy as jnp
from jax import lax
from jax.experimental import pallas as pl
from jax.experimental.pallas import tpu as pltpu

F32 = jnp.float32
BF16 = jnp.bfloat16
I32 = jnp.int32
EPS = 1e-6
HIGHEST = lax.Precision.HIGHEST
NT_DIMS = (((1,), (1,)), ((), ()))
TN_DIMS = (((0,), (0,)), ((), ()))

D_MODEL = 1024
HG_HEADS = 4
HG_DK = 128
HG_WIDTH = 512
HG_CHUNK = 64
HG_SUB = 16
HG_EXP_CLAMP = 80.0
MLA_HEADS = 4
MLA_Q_RANK = 384
MLA_KV_RANK = 256
MLA_NOPE = 128
MLA_ROPE = 64
MLA_V = 128
MLA_QK = MLA_NOPE + MLA_ROPE
MLA_QK_PAD = 256
ROPE_THETA = 10000.0
LRU_BLOCKS = 4
LRU_BW = 256
LRU_C = 8.0
N_EXPERTS = 16
CAPACITY = 2
LANES = 128
MIB = 1024 * 1024


def _cparams(semantics, vmem_mib):
    return pltpu.CompilerParams(dimension_semantics=semantics, vmem_limit_bytes=vmem_mib * MIB)


def _sigmoid(x):
    return 1.0 / (1.0 + jnp.exp(-x))


def _rms_scale(x, width):
    return lax.rsqrt(jnp.sum(x * x, axis=-1, keepdims=True) * (1.0 / width) + EPS)


def _norm_matmul_kernel(x_ref, g_ref, *refs, n_out):
    x = x_ref[...]
    h = (x * _rms_scale(x, x.shape[-1]) * g_ref[...]).astype(BF16)
    for w_ref, o_ref in zip(refs[:n_out], refs[n_out:]):
        o_ref[...] = jnp.dot(h, w_ref[...], preferred_element_type=F32).astype(o_ref.dtype)


def norm_matmul(x, g, ws, out_dtypes, tm):
    T, D = x.shape
    n = len(ws)
    in_specs = [pl.BlockSpec((tm, D), lambda i: (i, 0)), pl.BlockSpec((1, D), lambda i: (0, 0))]
    in_specs += [pl.BlockSpec(w.shape, lambda i: (0, 0)) for w in ws]
    out_specs = [pl.BlockSpec((tm, w.shape[1]), lambda i: (i, 0)) for w in ws]
    out_shape = [jax.ShapeDtypeStruct((T, w.shape[1]), dt) for w, dt in zip(ws, out_dtypes)]
    return pl.pallas_call(
        functools.partial(_norm_matmul_kernel, n_out=n),
        grid=(T // tm,), in_specs=in_specs, out_specs=out_specs, out_shape=out_shape,
        compiler_params=_cparams(("parallel",), 56), name="norm_matmul",
    )(x, g.reshape(1, D), *ws)


def _matmul_resid_kernel(r_ref, *refs, n_in):
    acc = r_ref[...]
    for a_ref, w_ref in zip(refs[:n_in], refs[n_in:2 * n_in]):
        acc = acc + jnp.dot(a_ref[...], w_ref[...], preferred_element_type=F32)
    refs[2 * n_in][...] = acc


def matmul_resid(resid, acts, ws, tm):
    T, D = resid.shape
    n = len(acts)
    in_specs = [pl.BlockSpec((tm, D), lambda i: (i, 0))]
    in_specs += [pl.BlockSpec((tm, a.shape[1]), lambda i: (i, 0)) for a in acts]
    in_specs += [pl.BlockSpec(w.shape, lambda i: (0, 0)) for w in ws]
    return pl.pallas_call(
        functools.partial(_matmul_resid_kernel, n_in=n),
        grid=(T // tm,), in_specs=in_specs, out_specs=pl.BlockSpec((tm, D), lambda i: (i, 0)),
        out_shape=jax.ShapeDtypeStruct((T, D), F32),
        compiler_params=_cparams(("parallel",), 40), name="matmul_resid",
    )(resid, *acts, *ws)


def _hgrn_kernel(*refs, reverse, final, n_chunk):
    if final:
        q_ref, v_ref, fz_ref, lb_ref, g_ref, of_ref, gain_ref, o_ref, st_ref = refs
    else:
        q_ref, v_ref, fz_ref, lb_ref, o_ref, st_ref = refs
    C, SB = HG_CHUNK, HG_SUB

    @pl.when(pl.program_id(1) == 0)
    def _():
        st_ref[...] = jnp.zeros_like(st_ref)

    row = lax.broadcasted_iota(I32, (C, C), 0)
    col = lax.broadcasted_iota(I32, (C, C), 1)
    blk_of_row = row // SB
    if reverse:
        tri = col >= row
        ref_of_row = blk_of_row * SB
        causal = col >= row
    else:
        tri = col <= row
        ref_of_row = blk_of_row * SB + (SB - 1)
        causal = col <= row
    ref_sum = (col >= ref_of_row) if reverse else (col <= ref_of_row)
    cum_mat = jnp.concatenate([jnp.where(tri, 1.0, 0.0), jnp.where(ref_sum, 1.0, 0.0)], axis=0)
    row_blk = lax.broadcasted_iota(I32, (C, HG_DK), 0) // SB

    def chunk(ci, carry):
        c = (n_chunk - 1 - ci) if reverse else ci
        r0 = pl.multiple_of(c * C, C)
        for h in range(HG_HEADS):
            cs = slice(h * HG_DK, (h + 1) * HG_DK)
            q = q_ref[pl.ds(r0, C), cs] * (HG_DK ** -0.5)
            v = v_ref[pl.ds(r0, C), cs]
            lb = lb_ref[:, cs]
            f = lb + (1.0 - lb) * _sigmoid(fz_ref[pl.ds(r0, C), cs])
            logf = jnp.log(f)
            k = 1.0 - f
            bb = jnp.dot(cum_mat, logf, precision=HIGHEST, preferred_element_type=F32)
            b, b_ref = bb[:C], bb[C:]
            kt = k * jnp.exp(b_ref - b)
            q_parts, k_parts = [], []
            for j in range(C // SB):
                jr = j * SB if reverse else j * SB + SB - 1
                q_parts.append((q * jnp.exp(jnp.minimum(b - b[jr:jr + 1, :], HG_EXP_CLAMP))).astype(BF16))
                k_parts.append(jnp.where(row_blk == j, kt, 0.0).astype(BF16))
            scores = lax.dot_general(jnp.concatenate(q_parts, axis=1), jnp.concatenate(k_parts, axis=1),
                                     NT_DIMS, preferred_element_type=F32)
            scores = jnp.where(causal, scores, 0.0).astype(BF16)
            b_end = b[0:1, :] if reverse else b[C - 1:C, :]
            state_t = st_ref[h]
            o = jnp.dot(scores, v.astype(BF16), preferred_element_type=F32)
            o = o + lax.dot_general((q * jnp.exp(b)).astype(BF16), state_t.astype(BF16), NT_DIMS,
                                    preferred_element_type=F32)
            k_hat = (k * jnp.exp(b_end - b)).astype(BF16)
            st_ref[h] = state_t * jnp.exp(b_end) + lax.dot_general(
                v.astype(BF16), k_hat, TN_DIMS, preferred_element_type=F32)
            if final:
                o = o + of_ref[pl.ds(r0, C), cs]
                y = o * _rms_scale(o, HG_DK) * gain_ref[...]
                g = g_ref[pl.ds(r0, C), cs]
                o_ref[pl.ds(r0, C), cs] = (y * (g * _sigmoid(g))).astype(o_ref.dtype)
            else:
                o_ref[pl.ds(r0, C), cs] = o
        return carry

    lax.fori_loop(0, n_chunk, chunk, 0)


def hgrn_direction(proj, lb, reverse, rows, o_fwd=None, gain=None):
    B, S, _ = proj.shape
    W = HG_WIDTH
    nblk = S // rows
    final = o_fwd is not None

    def rowmap(colblk):
        if reverse:
            return lambda b, c: (b, nblk - 1 - c, colblk)
        return lambda b, c: (b, c, colblk)

    blk = (None, rows, W)
    in_specs = [pl.BlockSpec(blk, rowmap(0)), pl.BlockSpec(blk, rowmap(1)),
                pl.BlockSpec(blk, rowmap(4 if reverse else 3)), pl.BlockSpec((1, W), lambda b, c: (0, 0))]
    args = [proj, proj, proj, lb]
    if final:
        in_specs += [pl.BlockSpec(blk, rowmap(2)), pl.BlockSpec(blk, rowmap(0)),
                     pl.BlockSpec((1, HG_DK), lambda b, c: (0, 0))]
        args += [proj, o_fwd, gain.reshape(1, HG_DK)]
    return pl.pallas_call(
        functools.partial(_hgrn_kernel, reverse=reverse, final=final, n_chunk=rows // HG_CHUNK),
        grid=(B, nblk), in_specs=in_specs, out_specs=pl.BlockSpec(blk, rowmap(0)),
        out_shape=jax.ShapeDtypeStruct((B, S, W), BF16 if final else F32),
        scratch_shapes=[pltpu.VMEM((HG_HEADS, HG_DK, HG_DK), F32)],
        compiler_params=_cparams(("parallel", "arbitrary"), 40),
        name="hgrn_bwd" if reverse else "hgrn_fwd",
    )(*args)


def _rope(x, cos_t, sin_a, sin_b):
    return x * cos_t + pltpu.roll(x, LANES - MLA_ROPE // 2, 1) * sin_a + pltpu.roll(x, MLA_ROPE // 2, 1) * sin_b


def _mla_prep_kernel(c_ref, qn_ref, kvn_ref, wq_ref, wkv_ref, qg_ref, kg_ref, cos_ref, sa_ref, sb_ref,
                     q_ref, k_ref, v_ref):
    c = c_ref[...]
    cq = c[:, :MLA_Q_RANK]
    ckv = c[:, MLA_Q_RANK:MLA_Q_RANK + MLA_KV_RANK]
    kpe = c[:, MLA_Q_RANK + MLA_KV_RANK:]
    qn = (cq * _rms_scale(cq, MLA_Q_RANK) * qn_ref[...]).astype(BF16)
    kvn = (ckv * _rms_scale(ckv, MLA_KV_RANK) * kvn_ref[...]).astype(BF16)
    qf = jnp.dot(qn, wq_ref[...], preferred_element_type=F32)
    kvf = jnp.dot(kvn, wkv_ref[...], preferred_element_type=F32)
    cos_t, sin_a, sin_b = cos_ref[...], sa_ref[...], sb_ref[...]
    qg, kg = qg_ref[...], kg_ref[...]
    sm_scale = MLA_QK ** -0.5
    kpe_ss = jnp.sum(kpe * kpe, axis=-1, keepdims=True)
    for h in range(MLA_HEADS):
        lo = h * MLA_QK_PAD
        q_nope, q_rope = qf[:, lo:lo + LANES], qf[:, lo + LANES:lo + 2 * LANES]
        qs = lax.rsqrt((jnp.sum(q_nope * q_nope, axis=-1, keepdims=True)
                        + jnp.sum(q_rope * q_rope, axis=-1, keepdims=True)) * (1.0 / MLA_QK) + EPS)
        q_ref[:, lo:lo + LANES] = (q_nope * qs * qg[:, :LANES] * sm_scale).astype(BF16)
        q_ref[:, lo + LANES:lo + 2 * LANES] = (
            _rope(q_rope * qs * qg[:, LANES:], cos_t, sin_a, sin_b) * sm_scale).astype(BF16)
        k_nope = kvf[:, lo:lo + LANES]
        ks = lax.rsqrt((jnp.sum(k_nope * k_nope, axis=-1, keepdims=True) + kpe_ss) * (1.0 / MLA_QK) + EPS)
        k_ref[:, lo:lo + LANES] = (k_nope * ks * kg[:, :LANES]).astype(BF16)
        k_ref[:, lo + LANES:lo + 2 * LANES] = _rope(kpe * ks * kg[:, LANES:], cos_t, sin_a, sin_b).astype(BF16)
        v_ref[:, h * MLA_V:(h + 1) * MLA_V] = kvf[:, lo + LANES:lo + 2 * LANES].astype(BF16)


def mla_prep(c, qn_g, kvn_g, wq, wkv, qg, kg, tables, seq, tm):
    T, CW = c.shape
    nseq = seq // tm
    full = lambda a: pl.BlockSpec(a.shape, lambda i: (0, 0))
    tab = pl.BlockSpec((tm, LANES), lambda i: (i % nseq, 0))
    HW = MLA_HEADS * MLA_QK_PAD
    return pl.pallas_call(
        _mla_prep_kernel, grid=(T // tm,),
        in_specs=[pl.BlockSpec((tm, CW), lambda i: (i, 0)), full(qn_g), full(kvn_g), full(wq), full(wkv),
                  full(qg), full(kg), tab, tab, tab],
        out_specs=[pl.BlockSpec((tm, HW), lambda i: (i, 0)), pl.BlockSpec((tm, HW), lambda i: (i, 0)),
                   pl.BlockSpec((tm, MLA_HEADS * MLA_V), lambda i: (i, 0))],
        out_shape=[jax.ShapeDtypeStruct((T, HW), BF16), jax.ShapeDtypeStruct((T, HW), BF16),
                   jax.ShapeDtypeStruct((T, MLA_HEADS * MLA_V), BF16)],
        compiler_params=_cparams(("parallel",), 40), name="mla_prep",
    )(c, qn_g, kvn_g, wq, wkv, qg, kg, *tables)


def _attn_kernel(q_ref, k_ref, v_ref, o_ref, m_ref, l_ref, acc_ref):
    ki = pl.program_id(3)

    @pl.when(ki == 0)
    def _():
        m_ref[...] = jnp.full_like(m_ref, -jnp.inf)
        l_ref[...] = jnp.zeros_like(l_ref)
        acc_ref[...] = jnp.zeros_like(acc_ref)

    s = lax.dot_general(q_ref[...], k_ref[...], NT_DIMS, preferred_element_type=F32)
    m_prev = m_ref[...]
    m_new = jnp.maximum(m_prev, jnp.max(s, axis=-1, keepdims=True))
    alpha = jnp.exp(m_prev - m_new)
    p = jnp.exp(s - m_new)
    l_ref[...] = alpha * l_ref[...] + jnp.sum(p, axis=-1, keepdims=True)
    acc_ref[...] = alpha * acc_ref[...] + jnp.dot(p.astype(BF16), v_ref[...], preferred_element_type=F32)
    m_ref[...] = m_new

    @pl.when(ki == pl.num_programs(3) - 1)
    def _():
        o_ref[...] = (acc_ref[...] / l_ref[...]).astype(o_ref.dtype)


def attention(q, k, v, tq, tk):
    B, S, _ = q.shape
    return pl.pallas_call(
        _attn_kernel, grid=(B, MLA_HEADS, S // tq, S // tk),
        in_specs=[pl.BlockSpec((None, tq, MLA_QK_PAD), lambda b, h, i, j: (b, i, h)),
                  pl.BlockSpec((None, tk, MLA_QK_PAD), lambda b, h, i, j: (b, j, h)),
                  pl.BlockSpec((None, tk, MLA_V), lambda b, h, i, j: (b, j, h))],
        out_specs=pl.BlockSpec((None, tq, MLA_V), lambda b, h, i, j: (b, i, h)),
        out_shape=jax.ShapeDtypeStruct((B, S, MLA_HEADS * MLA_V), BF16),
        scratch_shapes=[pltpu.VMEM((tq, 1), F32), pltpu.VMEM((tq, 1), F32), pltpu.VMEM((tq, MLA_V), F32)],
        compiler_params=_cparams(("parallel", "parallel", "parallel", "arbitrary"), 48), name="mla_attention",
    )(q, k, v)


def _scan_rows(a, u, reverse):
    ts = a.shape[0]
    rowi = lax.broadcasted_iota(I32, a.shape, 0)
    k = 1
    while k < ts:
        if reverse:
            a_s, u_s, valid = pltpu.roll(a, ts - k, 0), pltpu.roll(u, ts - k, 0), rowi < ts - k
        else:
            a_s, u_s, valid = pltpu.roll(a, k, 0), pltpu.roll(u, k, 0), rowi >= k
        u = u + a * jnp.where(valid, u_s, 0.0)
        a = a * jnp.where(valid, a_s, 1.0)
        k *= 2
    return a, u


def _lru_kernel(*refs, reverse, final):
    if final:
        (x_ref, prev_ref, next_ref, cw_ref, cb_ref, wa_ref, ba_ref, wx_ref, bx_ref, sp_ref,
         gate_ref, hf_ref, o_ref, carry_ref) = refs
    else:
        (x_ref, prev_ref, next_ref, cw_ref, cb_ref, wa_ref, ba_ref, wx_ref, bx_ref, sp_ref,
         o_ref, carry_ref) = refs
    step, nstep = pl.program_id(1), pl.num_programs(1)
    blk = (nstep - 1 - step) if reverse else step

    @pl.when(step == 0)
    def _():
        carry_ref[...] = jnp.zeros_like(carry_ref)

    x = x_ref[...]
    ts = x.shape[0]
    rowi = lax.broadcasted_iota(I32, x.shape, 0)
    has_prev = jnp.where(blk > 0, 1.0, 0.0)
    has_next = jnp.where(blk < nstep - 1, 1.0, 0.0)
    p6, p7, n0 = prev_ref[6:7, :] * has_prev, prev_ref[7:8, :] * has_prev, next_ref[0:1, :] * has_next
    xm1 = jnp.where(rowi == 0, p7, pltpu.roll(x, 1, 0))
    xm2 = jnp.where(rowi == 0, p6, jnp.where(rowi == 1, p7, pltpu.roll(x, 2, 0)))
    xp1 = jnp.where(rowi == ts - 1, n0, pltpu.roll(x, ts - 1, 0))
    cw = cw_ref[...]
    xc = xm2 * cw[0:1, :] + xm1 * cw[1:2, :] + x * cw[2:3, :] + xp1 * cw[3:4, :] + cb_ref[...]
    xcb = xc.astype(BF16)

    def block_diag(w_ref, b_ref):
        outs = [jnp.dot(xcb[:, n * LRU_BW:(n + 1) * LRU_BW], w_ref[n], preferred_element_type=F32)
                for n in range(LRU_BLOCKS)]
        return jnp.concatenate(outs, axis=1) + b_ref[...]

    r = _sigmoid(block_diag(wa_ref, ba_ref))
    i = _sigmoid(block_diag(wx_ref, bx_ref))
    log_a = (-LRU_C) * r * sp_ref[...]
    a = jnp.exp(log_a)
    u = jnp.sqrt(1.0 - jnp.exp(2.0 * log_a)) * (i * xc)
    a_cum, h = _scan_rows(a, u, reverse)
    h = h + a_cum * carry_ref[...]
    carry_ref[...] = h[0:1, :] if reverse else h[ts - 1:ts, :]
    if final:
        g = gate_ref[...]
        gelu = 0.5 * g * (1.0 + jnp.tanh(0.7978845608028654 * (g + 0.044715 * (g * g * g))))
        o_ref[...] = ((h + hf_ref[...]) * gelu).astype(o_ref.dtype)
    else:
        o_ref[...] = h


def lru_direction(proj, cw, cb, wa, ba, wx, bx, sp, reverse, ts, h_fwd=None):
    B, S, W2 = proj.shape
    W = W2 // 2
    nblk = S // ts
    final = h_fwd is not None
    sub = ts // 8

    def pos(c):
        return (nblk - 1 - c) if reverse else c

    tile = (None, ts, W)
    halo = (None, 8, W)
    vec = lambda r: pl.BlockSpec((r, W), lambda b, c: (0, 0))
    wspec = pl.BlockSpec((LRU_BLOCKS, LRU_BW, LRU_BW), lambda b, c: (0, 0, 0))
    in_specs = [pl.BlockSpec(tile, lambda b, c: (b, pos(c), 1)),
                pl.BlockSpec(halo, lambda b, c: (b, jnp.maximum(pos(c) * sub - 1, 0), 1)),
                pl.BlockSpec(halo, lambda b, c: (b, jnp.minimum((pos(c) + 1) * sub, S // 8 - 1), 1)),
                vec(4), vec(1), wspec, vec(1), wspec, vec(1), vec(1)]
    args = [proj, proj, proj, cw, cb, wa, ba, wx, bx, sp]
    if final:
        in_specs += [pl.BlockSpec(tile, lambda b, c: (b, pos(c), 0)), pl.BlockSpec(tile, lambda b, c: (b, pos(c), 0))]
        args += [proj, h_fwd]
    return pl.pallas_call(
        functools.partial(_lru_kernel, reverse=reverse, final=final),
        grid=(B, nblk), in_specs=in_specs, out_specs=pl.BlockSpec(tile, lambda b, c: (b, pos(c), 0)),
        out_shape=jax.ShapeDtypeStruct((B, S, W), BF16 if final else F32),
        scratch_shapes=[pltpu.VMEM((1, W), F32)],
        compiler_params=_cparams(("parallel", "arbitrary"), 48),
        name="lru_bwd" if reverse else "lru_fwd",
    )(*args)


def _router_kernel(x_ref, g_ref, wr_ref, aff_ref):
    x = x_ref[...]
    xn = x * _rms_scale(x, x.shape[-1]) * g_ref[...]
    logits = lax.dot_general(wr_ref[...], xn, NT_DIMS, precision=HIGHEST, preferred_element_type=F32)
    e = jnp.exp(logits - jnp.max(logits, axis=0, keepdims=True))
    aff_ref[...] = e / jnp.sum(e, axis=0, keepdims=True)


def router(x, g, wr_t, tm):
    N, D = x.shape
    E = wr_t.shape[0]
    return pl.pallas_call(
        _router_kernel, grid=(N // tm,),
        in_specs=[pl.BlockSpec((tm, D), lambda i: (i, 0)), pl.BlockSpec((1, D), lambda i: (0, 0)),
                  pl.BlockSpec((E, D), lambda i: (0, 0))],
        out_specs=pl.BlockSpec((E, tm), lambda i: (0, i)),
        out_shape=jax.ShapeDtypeStruct((E, N), F32),
        compiler_params=_cparams(("parallel",), 32), name="moe_router",
    )(x, g.reshape(1, D), wr_t)


def _topk_kernel(aff_ref, pos_ref, off_ref, cnt_ref, *, cap):
    bits = pltpu.bitcast(aff_ref[...], I32)
    nb = bits.shape[0]

    def count(pred):
        return jnp.sum(jnp.sum(jnp.where(pred, 1.0, 0.0), axis=1, keepdims=True), axis=0, keepdims=True)

    def bisect(i, lo):
        cand = lo | lax.shift_left(jnp.int32(1), jnp.int32(30) - i)
        return jnp.where(count(bits >= cand) >= cap, cand, lo)

    thr = lax.fori_loop(0, 31, bisect, jnp.zeros((1, 1), I32))

    r128 = lax.broadcasted_iota(I32, (LANES, LANES), 0)
    c128 = lax.broadcasted_iota(I32, (LANES, LANES), 1)
    upper = jnp.where(r128 <= c128, 1.0, 0.0).astype(BF16)
    rn = lax.broadcasted_iota(I32, (nb, nb), 0)
    cn = lax.broadcasted_iota(I32, (nb, nb), 1)
    strict_lower = jnp.where(cn < rn, 1.0, 0.0).astype(BF16)

    def running_count(m):
        incl = jnp.dot(m.astype(BF16), upper, preferred_element_type=F32)
        tot = jnp.broadcast_to(incl[:, LANES - 1:LANES], (nb, LANES)).astype(BF16)
        return incl, jnp.dot(strict_lower, tot, preferred_element_type=F32)

    gt = jnp.where(bits > thr, 1.0, 0.0)
    eq = jnp.where(bits == thr, 1.0, 0.0)
    need = cap - count(bits > thr)
    eq_incl, eq_off = running_count(eq)
    sel = gt + eq * jnp.where(eq_incl - eq + eq_off < need, 1.0, 0.0)
    incl, off = running_count(sel)
    cnt = incl + off
    pos_ref[...] = jnp.where(sel > 0.0, cnt - 1.0, -1.0).astype(I32)
    off_ref[...] = off.astype(I32)
    cnt_ref[...] = cnt


def topk_slots(aff3, cap):
    E, NB, _ = aff3.shape
    spec = pl.BlockSpec((None, NB, LANES), lambda e: (e, 0, 0))
    return pl.pallas_call(
        functools.partial(_topk_kernel, cap=cap), grid=(E,), in_specs=[spec], out_specs=[spec, spec, spec],
        out_shape=[jax.ShapeDtypeStruct((E, NB, LANES), I32), jax.ShapeDtypeStruct((E, NB, LANES), I32),
                   jax.ShapeDtypeStruct((E, NB, LANES), F32)],
        compiler_params=_cparams(("parallel",), 32), name="moe_topk",
    )(aff3)


def _compact_kernel(blo_ref, bhi_ref, cnt_ref, idx_ref, *, n_chunk):
    e = pl.program_id(0)
    slot = lax.broadcasted_iota(I32, (LANES, LANES), 0)

    def chunk(c, carry):
        lo, hi = blo_ref[e, c], bhi_ref[e, c]
        sigma = (slot + c * LANES).astype(F32)

        def blk(b, acc):
            return acc + jnp.where(cnt_ref[b] <= sigma, 1.0, 0.0)

        acc = lax.fori_loop(lo, hi + 1, blk, jnp.zeros((LANES, LANES), F32))
        col = jnp.sum(acc, axis=1, keepdims=True) + (lo * LANES).astype(F32)
        idx_ref[c] = jnp.broadcast_to(col, (LANES, LANES)).T[0:1, :].astype(I32)
        return carry

    lax.fori_loop(0, n_chunk, chunk, 0)


def compact_slots(cnt3, cap):
    E, NB, _ = cnt3.shape
    n_chunk = cap // LANES
    starts = jnp.arange(n_chunk, dtype=F32) * LANES
    blo = jnp.sum(cnt3[:, :, LANES - 1, None] <= starts[None, None, :], axis=1).astype(I32)
    bhi = jnp.sum(cnt3[:, :, 0, None] <= starts[None, None, :] + (LANES - 1), axis=1).astype(I32) - 1
    grid_spec = pltpu.PrefetchScalarGridSpec(
        num_scalar_prefetch=2, grid=(E,),
        in_specs=[pl.BlockSpec((None, NB, 1, LANES), lambda e, *_: (e, 0, 0, 0))],
        out_specs=pl.BlockSpec((None, n_chunk, 1, LANES), lambda e, *_: (e, 0, 0, 0)))
    idx = pl.pallas_call(
        functools.partial(_compact_kernel, n_chunk=n_chunk), grid_spec=grid_spec,
        out_shape=jax.ShapeDtypeStruct((E, n_chunk, 1, LANES), I32),
        compiler_params=_cparams(("arbitrary",), 32), name="moe_compact",
    )(blo, bhi, cnt3.reshape(E, NB, 1, LANES))
    return idx.reshape(E, cap)


def _ffn_kernel(idx_ref, x_hbm, g_ref, wg_ref, wu_ref, wd_ref, o_ref, rows_ref, sem):
    tm = rows_ref.shape[0]

    def issue(r, carry):
        pltpu.make_async_copy(x_hbm.at[pl.ds(idx_ref[0, 0, 0, r], 1)], rows_ref.at[pl.ds(r, 1)], sem).start()
        return carry

    lax.fori_loop(0, tm, issue, 0)
    pltpu.make_async_copy(x_hbm.at[pl.ds(0, tm)], rows_ref, sem).wait()
    x = rows_ref[...]
    xn = (x * _rms_scale(x, x.shape[-1]) * g_ref[...]).astype(BF16)
    gate = jnp.dot(xn, wg_ref[...], preferred_element_type=F32)
    up = jnp.dot(xn, wu_ref[...], preferred_element_type=F32)
    hid = (gate * _sigmoid(gate) * up).astype(BF16)
    o_ref[...] = jnp.dot(hid, wd_ref[...], preferred_element_type=F32).astype(o_ref.dtype)


def expert_ffn(idx, x, g, wg, wu, wd, tm):
    E, cap = idx.shape
    N, D = x.shape
    FF = wg.shape[2]
    return pl.pallas_call(
        _ffn_kernel, grid=(E, cap // tm),
        in_specs=[pl.BlockSpec((1, 1, 1, tm), lambda e, c: (e, c, 0, 0), memory_space=pltpu.SMEM),
                  pl.BlockSpec(memory_space=pl.ANY),
                  pl.BlockSpec((1, D), lambda e, c: (0, 0)),
                  pl.BlockSpec((None, D, FF), lambda e, c: (e, 0, 0)),
                  pl.BlockSpec((None, D, FF), lambda e, c: (e, 0, 0)),
                  pl.BlockSpec((None, FF, D), lambda e, c: (e, 0, 0))],
        out_specs=pl.BlockSpec((None, tm, D), lambda e, c: (e, c, 0)),
        out_shape=jax.ShapeDtypeStruct((E, cap, D), BF16),
        scratch_shapes=[pltpu.VMEM((tm, D), F32), pltpu.SemaphoreType.DMA(())],
        compiler_params=_cparams(("parallel", "arbitrary"), 56), name="moe_ffn",
    )(idx.reshape(E, cap // tm, 1, tm), x, g.reshape(1, D), wg, wu, wd)


def _combine_kernel(s0_ref, x_ref, pos_ref, aff_ref, ye_hbm, o_ref, buf_ref, sem, *, win):
    t = pl.program_id(0)
    n_exp = pos_ref.shape[1]

    def window(e, slot):
        s0 = pl.multiple_of(s0_ref[e, t], 16)
        return pltpu.make_async_copy(ye_hbm.at[e, pl.ds(s0, win)], buf_ref.at[slot], sem.at[slot])

    window(0, 0).start()
    acc = x_ref[...]
    lane = lax.broadcasted_iota(I32, (acc.shape[0], win), 1)
    for e in range(n_exp):
        slot = e % 2
        if e + 1 < n_exp:
            window(e + 1, 1 - slot).start()
        window(e, slot).wait()
        onehot = jnp.where(pos_ref[:, e:e + 1] == lane + s0_ref[e, t], 1.0, 0.0).astype(BF16)
        acc = acc + jnp.dot(onehot, buf_ref[slot], preferred_element_type=F32) * aff_ref[:, e:e + 1]
    o_ref[...] = acc


def moe_combine(x, pos_t, aff_t, off, ye, tt):
    N, D = x.shape
    E, cap, _ = ye.shape
    win = tt + 16
    s0 = jnp.clip((off // 16) * 16, 0, cap - win).astype(I32)
    grid_spec = pltpu.PrefetchScalarGridSpec(
        num_scalar_prefetch=1, grid=(N // tt,),
        in_specs=[pl.BlockSpec((tt, D), lambda t, *_: (t, 0)), pl.BlockSpec((tt, E), lambda t, *_: (t, 0)),
                  pl.BlockSpec((tt, E), lambda t, *_: (t, 0)), pl.BlockSpec(memory_space=pl.ANY)],
        out_specs=pl.BlockSpec((tt, D), lambda t, *_: (t, 0)),
        scratch_shapes=[pltpu.VMEM((2, win, D), BF16), pltpu.SemaphoreType.DMA((2,))])
    return pl.pallas_call(
        functools.partial(_combine_kernel, win=win), grid_spec=grid_spec,
        out_shape=jax.ShapeDtypeStruct((N, D), F32),
        compiler_params=_cparams(("arbitrary",), 32), name="moe_combine",
    )(s0, x, pos_t, aff_t, ye)


def expert_choice_moe(x, g, wr_t, wg, wu, wd):
    N, D = x.shape
    E = wr_t.shape[0]
    cap = CAPACITY * N // E
    nb = N // LANES
    aff = router(x, g, wr_t, 512)
    pos3, off3, cnt3 = topk_slots(aff.reshape(E, nb, LANES), cap)
    idx = compact_slots(cnt3, cap)
    ye = expert_ffn(idx, x, g, wg, wu, wd, 512)
    return moe_combine(x, pos3.reshape(E, N).T, aff.T, off3[:, :, 0], ye, LANES)


def _rope_tables(seq):
    pos = jnp.arange(seq, dtype=F32)
    inv = 1.0 / (ROPE_THETA ** (jnp.arange(0, MLA_ROPE, 2, dtype=F32) / MLA_ROPE))
    ang = pos[:, None] * inv[None, :]
    cos, sin = jnp.cos(ang), jnp.sin(ang)
    z32, z64 = jnp.zeros_like(cos), jnp.zeros((seq, LANES - MLA_ROPE), F32)
    return (jnp.concatenate([cos, cos, z64], axis=1), jnp.concatenate([-sin, z32, z64], axis=1),
            jnp.concatenate([z32, sin, z64], axis=1))


def _pad_cols(a, width):
    return jnp.pad(a, [(0, 0)] * (a.ndim - 1) + [(0, width - a.shape[-1])])


def even_layer(x, batch, seq, g_mix, w_in, w_out, lb, hg_gain, qn_g, kvn_g, w_uq, w_ukv, q_gain, k_gain):
    T, D = x.shape
    hgw = 5 * HG_WIDTH
    w_hg = w_in[:, :hgw].astype(BF16)
    w_mla = _pad_cols(w_in[:, hgw:], 6 * LANES).astype(BF16)
    proj_hg, proj_mla = norm_matmul(x, g_mix, [w_hg, w_mla], [F32, F32], 256)

    proj3 = proj_hg.reshape(batch, seq, hgw)
    rows = 256
    o_f = hgrn_direction(proj3, lb[0:1], False, rows)
    o_hg = hgrn_direction(proj3, lb[1:2], True, rows, o_fwd=o_f, gain=hg_gain)

    wq = _pad_cols(w_uq.reshape(MLA_Q_RANK, MLA_HEADS, MLA_QK), MLA_QK_PAD).reshape(MLA_Q_RANK, -1).astype(BF16)
    wkv = w_ukv.astype(BF16)
    qg = _pad_cols(q_gain.reshape(1, MLA_QK), MLA_QK_PAD)
    kg = _pad_cols(k_gain.reshape(1, MLA_QK), MLA_QK_PAD)
    q, k, v = mla_prep(proj_mla, qn_g.reshape(1, -1), kvn_g.reshape(1, -1), wq, wkv, qg, kg,
                       _rope_tables(seq), seq, 512)
    hq = MLA_HEADS * MLA_QK_PAD
    tq = tk = min(seq, 1024)
    o_mla = attention(q.reshape(batch, seq, hq), k.reshape(batch, seq, hq),
                      v.reshape(batch, seq, MLA_HEADS * MLA_V), tq, tk)

    w_o = w_out.astype(BF16)
    return matmul_resid(x, [o_hg.reshape(T, HG_WIDTH), o_mla.reshape(T, MLA_HEADS * MLA_V)],
                        [w_o[:HG_WIDTH], w_o[HG_WIDTH:]], 512)


def odd_layer(x, batch, seq, g_mix, w_in, conv_w, conv_b, wa, ba, wx, bx, lam, w_out):
    T, D = x.shape
    (proj,) = norm_matmul(x, g_mix, [w_in.astype(BF16)], [F32], 256)
    proj3 = proj.reshape(batch, seq, 2 * D)
    sp = jax.nn.softplus(-lam.astype(F32))
    cb = conv_b.reshape(1, D)
    ts = 128
    h_f = lru_direction(proj3, conv_w, cb, wa[0].astype(BF16), ba[0:1], wx[0].astype(BF16), bx[0:1], sp[0:1],
                        False, ts)
    y = lru_direction(proj3, conv_w, cb, wa[1].astype(BF16), ba[1:2], wx[1].astype(BF16), bx[1:2], sp[1:2],
                      True, ts, h_fwd=h_f)
    return matmul_resid(x, [y.reshape(T, D)], [w_out.astype(BF16)], 512)


def _trunk(x3, norm_mix, norm_ffn, ev_w_in, ev_w_out, hg_lb_logits, hg_out_gain, mla_q_norm, mla_kv_norm,
           mla_w_uq, mla_w_ukv, mla_q_gain, mla_k_gain, od_w_in, od_conv_w, od_conv_b, rg_w_a, rg_b_a,
           rg_w_x, rg_b_x, rg_lambda, od_w_out, moe_router, moe_w_gate, moe_w_up, moe_w_down):
    batch, seq, D = x3.shape
    x = x3.reshape(batch * seq, D)
    lb_all = jnp.cumsum(jax.nn.softmax(hg_lb_logits.astype(F32), axis=1), axis=1)
    depth = norm_mix.shape[0]
    for layer in range(depth):
        j = layer // 2
        if layer % 2 == 0:
            x = even_layer(x, batch, seq, norm_mix[layer], ev_w_in[j], ev_w_out[j], lb_all[:, j],
                           hg_out_gain[j], mla_q_norm[j], mla_kv_norm[j], mla_w_uq[j], mla_w_ukv[j],
                           mla_q_gain[j], mla_k_gain[j])
        else:
            x = odd_layer(x, batch, seq, norm_mix[layer], od_w_in[j], od_conv_w[j], od_conv_b[j], rg_w_a[j],
                          rg_b_a[j], rg_w_x[j], rg_b_x[j], rg_lambda[j], od_w_out[j])
        x = expert_choice_moe(x, norm_ffn[layer], moe_router[layer].T, moe_w_gate[layer].astype(BF16),
                              moe_w_up[layer].astype(BF16), moe_w_down[layer].astype(BF16))
    return x.reshape(batch, seq, D)


def kernel(x_prompt, x_sample, norm_mix, norm_ffn, ev_w_in, ev_w_out, hg_lb_logits, hg_out_gain, mla_q_norm,
           mla_kv_norm, mla_w_uq, mla_w_ukv, mla_q_gain, mla_k_gain, od_w_in, od_conv_w, od_conv_b, rg_w_a,
           rg_b_a, rg_w_x, rg_b_x, rg_lambda, od_w_out, moe_router, moe_w_gate, moe_w_up, moe_w_down):
    params = (norm_mix, norm_ffn, ev_w_in, ev_w_out, hg_lb_logits, hg_out_gain, mla_q_norm, mla_kv_norm,
              mla_w_uq, mla_w_ukv, mla_q_gain, mla_k_gain, od_w_in, od_conv_w, od_conv_b, rg_w_a, rg_b_a,
              rg_w_x, rg_b_x, rg_lambda, od_w_out, moe_router, moe_w_gate, moe_w_up, moe_w_down)
    return (_trunk(x_prompt, *params), _trunk(x_sample, *params))
```

```python
import functools

import jax
import jax.numpy as jnp
from jax import lax
from jax.experimental import pallas as pl
from jax.experimental.pallas import tpu as pltpu

F32 = jnp.float32
BF16 = jnp.bfloat16
I32 = jnp.int32
EPS = 1e-6
HIGHEST = lax.Precision.HIGHEST
NT_DIMS = (((1,), (1,)), ((), ()))
TN_DIMS = (((0,), (0,)), ((), ()))

D_MODEL = 1024
HG_HEADS = 4
HG_DK = 128
HG_WIDTH = 512
HG_CHUNK = 64
HG_SUB = 16
HG_EXP_CLAMP = 80.0
MLA_HEADS = 4
MLA_Q_RANK = 384
MLA_KV_RANK = 256
MLA_NOPE = 128
MLA_ROPE = 64
MLA_V = 128
MLA_QK = MLA_NOPE + MLA_ROPE
MLA_QK_PAD = 256
ROPE_THETA = 10000.0
LRU_BLOCKS = 4
LRU_BW = 256
LRU_C = 8.0
N_EXPERTS = 16
CAPACITY = 2
LANES = 128
LOG2_E = 1.4426950408889634
MIB = 1024 * 1024


def _cparams(semantics, vmem_mib):
    return pltpu.CompilerParams(dimension_semantics=semantics, vmem_limit_bytes=vmem_mib * MIB)


def _sigmoid(x):
    return 1.0 / (1.0 + jnp.exp(-x))


def _rms_scale(x, width):
    return lax.rsqrt(jnp.sum(x * x, axis=-1, keepdims=True) * (1.0 / width) + EPS)


def _norm_matmul_kernel(x_ref, g_ref, *refs, n_out):
    x = x_ref[...]
    h = (x * _rms_scale(x, x.shape[-1]) * g_ref[...]).astype(BF16)
    for w_ref, o_ref in zip(refs[:n_out], refs[n_out:]):
        o_ref[...] = jnp.dot(h, w_ref[...], preferred_element_type=F32).astype(o_ref.dtype)


def norm_matmul(x, g, ws, out_dtypes, tm):
    T, D = x.shape
    n = len(ws)
    in_specs = [pl.BlockSpec((tm, D), lambda i: (i, 0)), pl.BlockSpec((1, D), lambda i: (0, 0))]
    in_specs += [pl.BlockSpec(w.shape, lambda i: (0, 0)) for w in ws]
    out_specs = [pl.BlockSpec((tm, w.shape[1]), lambda i: (i, 0)) for w in ws]
    out_shape = [jax.ShapeDtypeStruct((T, w.shape[1]), dt) for w, dt in zip(ws, out_dtypes)]
    return pl.pallas_call(
        functools.partial(_norm_matmul_kernel, n_out=n),
        grid=(T // tm,), in_specs=in_specs, out_specs=out_specs, out_shape=out_shape,
        compiler_params=_cparams(("parallel",), 56), name="norm_matmul",
    )(x, g.reshape(1, D), *ws)


def _matmul_resid_kernel(r_ref, *refs, n_in):
    acc = r_ref[...]
    for a_ref, w_ref in zip(refs[:n_in], refs[n_in:2 * n_in]):
        acc = acc + jnp.dot(a_ref[...], w_ref[...], preferred_element_type=F32)
    refs[2 * n_in][...] = acc


def matmul_resid(resid, acts, ws, tm):
    T, D = resid.shape
    n = len(acts)
    in_specs = [pl.BlockSpec((tm, D), lambda i: (i, 0))]
    in_specs += [pl.BlockSpec((tm, a.shape[1]), lambda i: (i, 0)) for a in acts]
    in_specs += [pl.BlockSpec(w.shape, lambda i: (0, 0)) for w in ws]
    return pl.pallas_call(
        functools.partial(_matmul_resid_kernel, n_in=n),
        grid=(T // tm,), in_specs=in_specs, out_specs=pl.BlockSpec((tm, D), lambda i: (i, 0)),
        out_shape=jax.ShapeDtypeStruct((T, D), F32),
        compiler_params=_cparams(("parallel",), 40), name="matmul_resid",
    )(resid, *acts, *ws)


def _hgrn_kernel(*refs, reverse, final, n_chunk):
    if final:
        q_ref, v_ref, fz_ref, lb_ref, g_ref, of_ref, gain_ref, o_ref, st_ref = refs
    else:
        q_ref, v_ref, fz_ref, lb_ref, o_ref, st_ref = refs
    C, SB = HG_CHUNK, HG_SUB

    @pl.when(pl.program_id(1) == 0)
    def _():
        st_ref[...] = jnp.zeros_like(st_ref)

    row = lax.broadcasted_iota(I32, (C, C), 0)
    col = lax.broadcasted_iota(I32, (C, C), 1)
    blk_of_row = row // SB
    if reverse:
        tri = col >= row
        ref_of_row = blk_of_row * SB
        causal = col >= row
    else:
        tri = col <= row
        ref_of_row = blk_of_row * SB + (SB - 1)
        causal = col <= row
    ref_sum = (col >= ref_of_row) if reverse else (col <= ref_of_row)
    cum_mat = jnp.concatenate([jnp.where(tri, 1.0, 0.0), jnp.where(ref_sum, 1.0, 0.0)], axis=0)
    row_blk = lax.broadcasted_iota(I32, (C, HG_WIDTH), 0) // SB

    heads = [slice(h * HG_DK, (h + 1) * HG_DK) for h in range(HG_HEADS)]
    order = [((n_chunk - 1 - ci) if reverse else ci) for ci in range(n_chunk)]
    lb = lb_ref[...]

    per_chunk = []
    for c in order:
        rows = slice(c * C, (c + 1) * C)
        q = q_ref[rows, :] * (HG_DK ** -0.5)
        f = lb + (1.0 - lb) * _sigmoid(fz_ref[rows, :])
        logf = jnp.log(f)
        k = 1.0 - f
        bb = jnp.dot(cum_mat, logf, precision=HIGHEST, preferred_element_type=F32)
        b, b_blk = bb[:C], bb[C:]
        kt = k * jnp.exp(b_blk - b)
        q_parts, k_parts = [], []
        for j in range(C // SB):
            jr = j * SB if reverse else j * SB + SB - 1
            q_parts.append((q * jnp.exp(jnp.minimum(b - b[jr:jr + 1, :], HG_EXP_CLAMP))).astype(BF16))
            k_parts.append(jnp.where(row_blk == j, kt, 0.0).astype(BF16))
        b_end = b[0:1, :] if reverse else b[C - 1:C, :]
        vb = v_ref[rows, :].astype(BF16)
        intra = []
        for cs in heads:
            scores = lax.dot_general(jnp.concatenate([p[:, cs] for p in q_parts], axis=1),
                                     jnp.concatenate([p[:, cs] for p in k_parts], axis=1),
                                     NT_DIMS, preferred_element_type=F32)
            scores = jnp.where(causal, scores, 0.0).astype(BF16)
            intra.append(jnp.dot(scores, vb[:, cs], preferred_element_type=F32))
        per_chunk.append((rows, intra, (q * jnp.exp(b)).astype(BF16), (k * jnp.exp(b_end - b)).astype(BF16),
                          jnp.exp(b_end), vb))

    states = [st_ref[h] for h in range(HG_HEADS)]
    for rows, intra, q_dec, k_hat, decay_end, vb in per_chunk:
        for h, cs in enumerate(heads):
            o = intra[h] + lax.dot_general(q_dec[:, cs], states[h].astype(BF16), NT_DIMS,
                                           preferred_element_type=F32)
            states[h] = states[h] * decay_end[:, cs] + lax.dot_general(
                vb[:, cs], k_hat[:, cs], TN_DIMS, preferred_element_type=F32)
            if final:
                o = o + of_ref[rows, cs]
                y = o * _rms_scale(o, HG_DK) * gain_ref[...]
                g = g_ref[rows, cs]
                o_ref[rows, cs] = (y * (g * _sigmoid(g))).astype(o_ref.dtype)
            else:
                o_ref[rows, cs] = o
    for h in range(HG_HEADS):
        st_ref[h] = states[h]


def hgrn_direction(proj, lb, reverse, rows, o_fwd=None, gain=None):
    B, S, _ = proj.shape
    W = HG_WIDTH
    nblk = S // rows
    final = o_fwd is not None

    def rowmap(colblk):
        if reverse:
            return lambda b, c: (b, nblk - 1 - c, colblk)
        return lambda b, c: (b, c, colblk)

    blk = (None, rows, W)
    in_specs = [pl.BlockSpec(blk, rowmap(0)), pl.BlockSpec(blk, rowmap(1)),
                pl.BlockSpec(blk, rowmap(4 if reverse else 3)), pl.BlockSpec((1, W), lambda b, c: (0, 0))]
    args = [proj, proj, proj, lb]
    if final:
        in_specs += [pl.BlockSpec(blk, rowmap(2)), pl.BlockSpec(blk, rowmap(0)),
                     pl.BlockSpec((1, HG_DK), lambda b, c: (0, 0))]
        args += [proj, o_fwd, gain.reshape(1, HG_DK)]
    return pl.pallas_call(
        functools.partial(_hgrn_kernel, reverse=reverse, final=final, n_chunk=rows // HG_CHUNK),
        grid=(B, nblk), in_specs=in_specs, out_specs=pl.BlockSpec(blk, rowmap(0)),
        out_shape=jax.ShapeDtypeStruct((B, S, W), BF16 if final else F32),
        scratch_shapes=[pltpu.VMEM((HG_HEADS, HG_DK, HG_DK), F32)],
        compiler_params=_cparams(("parallel", "arbitrary"), 40),
        name="hgrn_bwd" if reverse else "hgrn_fwd",
    )(*args)


def _rope(x, cos_t, sin_a, sin_b):
    return x * cos_t + pltpu.roll(x, LANES - MLA_ROPE // 2, 1) * sin_a + pltpu.roll(x, MLA_ROPE // 2, 1) * sin_b


def _mla_prep_kernel(c_ref, qn_ref, kvn_ref, wq_ref, wkv_ref, qg_ref, kg_ref, cos_ref, sa_ref, sb_ref,
                     q_ref, k_ref, v_ref):
    c = c_ref[...]
    cq = c[:, :MLA_Q_RANK]
    ckv = c[:, MLA_Q_RANK:MLA_Q_RANK + MLA_KV_RANK]
    kpe = c[:, MLA_Q_RANK + MLA_KV_RANK:]
    qn = (cq * _rms_scale(cq, MLA_Q_RANK) * qn_ref[...]).astype(BF16)
    kvn = (ckv * _rms_scale(ckv, MLA_KV_RANK) * kvn_ref[...]).astype(BF16)
    qf = jnp.dot(qn, wq_ref[...], preferred_element_type=F32)
    kvf = jnp.dot(kvn, wkv_ref[...], preferred_element_type=F32)
    cos_t, sin_a, sin_b = cos_ref[...], sa_ref[...], sb_ref[...]
    qg, kg = qg_ref[...], kg_ref[...]
    sm_scale = (MLA_QK ** -0.5) * LOG2_E
    kpe_ss = jnp.sum(kpe * kpe, axis=-1, keepdims=True)
    for h in range(MLA_HEADS):
        lo = h * MLA_QK_PAD
        q_nope, q_rope = qf[:, lo:lo + LANES], qf[:, lo + LANES:lo + 2 * LANES]
        qs = lax.rsqrt((jnp.sum(q_nope * q_nope, axis=-1, keepdims=True)
                        + jnp.sum(q_rope * q_rope, axis=-1, keepdims=True)) * (1.0 / MLA_QK) + EPS)
        q_ref[:, lo:lo + LANES] = (q_nope * qs * qg[:, :LANES] * sm_scale).astype(BF16)
        q_ref[:, lo + LANES:lo + 2 * LANES] = (
            _rope(q_rope * qs * qg[:, LANES:], cos_t, sin_a, sin_b) * sm_scale).astype(BF16)
        k_nope = kvf[:, lo:lo + LANES]
        ks = lax.rsqrt((jnp.sum(k_nope * k_nope, axis=-1, keepdims=True) + kpe_ss) * (1.0 / MLA_QK) + EPS)
        k_ref[:, lo:lo + LANES] = (k_nope * ks * kg[:, :LANES]).astype(BF16)
        k_ref[:, lo + LANES:lo + 2 * LANES] = _rope(kpe * ks * kg[:, LANES:], cos_t, sin_a, sin_b).astype(BF16)
        v_ref[:, lo:lo + LANES] = kvf[:, lo + LANES:lo + 2 * LANES].astype(BF16)
        v_ref[:, lo + LANES:lo + 2 * LANES] = jnp.ones((c.shape[0], LANES), BF16)


def mla_prep(c, qn_g, kvn_g, wq, wkv, qg, kg, tables, seq, tm):
    T, CW = c.shape
    nseq = seq // tm
    full = lambda a: pl.BlockSpec(a.shape, lambda i: (0, 0))
    tab = pl.BlockSpec((tm, LANES), lambda i: (i % nseq, 0))
    HW = MLA_HEADS * MLA_QK_PAD
    return pl.pallas_call(
        _mla_prep_kernel, grid=(T // tm,),
        in_specs=[pl.BlockSpec((tm, CW), lambda i: (i, 0)), full(qn_g), full(kvn_g), full(wq), full(wkv),
                  full(qg), full(kg), tab, tab, tab],
        out_specs=[pl.BlockSpec((tm, HW), lambda i: (i, 0))] * 3,
        out_shape=[jax.ShapeDtypeStruct((T, HW), BF16)] * 3,
        compiler_params=_cparams(("parallel",), 40), name="mla_prep",
    )(c, qn_g, kvn_g, wq, wkv, qg, kg, *tables)


def _attn_kernel(q_ref, k_ref, v_ref, o_ref, m_ref, l_ref, acc_ref):
    ki = pl.program_id(3)

    @pl.when(ki == 0)
    def _():
        m_ref[...] = jnp.full_like(m_ref, -jnp.inf)
        l_ref[...] = jnp.zeros_like(l_ref)
        acc_ref[...] = jnp.zeros_like(acc_ref)

    s = lax.dot_general(q_ref[...], k_ref[...], NT_DIMS, preferred_element_type=F32)
    m_prev = m_ref[...]
    m_new = jnp.maximum(m_prev, jnp.max(s, axis=-1, keepdims=True))
    alpha = jnp.exp2(m_prev - m_new)
    p = jnp.exp2(s - m_new)
    l_ref[...] = alpha * l_ref[...] + jnp.sum(p, axis=-1, keepdims=True)
    acc_ref[...] = alpha * acc_ref[...] + jnp.dot(p.astype(BF16), v_ref[...], preferred_element_type=F32)
    m_ref[...] = m_new

    @pl.when(ki == pl.num_programs(3) - 1)
    def _():
        o_ref[...] = (acc_ref[...] / l_ref[...]).astype(o_ref.dtype)


def attention(q, k, v, tq, tk):
    B, S, _ = q.shape
    return pl.pallas_call(
        _attn_kernel, grid=(B, MLA_HEADS, S // tq, S // tk),
        in_specs=[pl.BlockSpec((None, tq, MLA_QK_PAD), lambda b, h, i, j: (b, i, h)),
                  pl.BlockSpec((None, tk, MLA_QK_PAD), lambda b, h, i, j: (b, j, h)),
                  pl.BlockSpec((None, tk, MLA_V), lambda b, h, i, j: (b, j, 2 * h))],
        out_specs=pl.BlockSpec((None, tq, MLA_V), lambda b, h, i, j: (b, i, h)),
        out_shape=jax.ShapeDtypeStruct((B, S, MLA_HEADS * MLA_V), BF16),
        scratch_shapes=[pltpu.VMEM((tq, 1), F32), pltpu.VMEM((tq, 1), F32), pltpu.VMEM((tq, MLA_V), F32)],
        compiler_params=_cparams(("parallel", "parallel", "parallel", "arbitrary"), 48), name="mla_attention",
    )(q, k, v)


SUBLANES = 8


def _scan_rows(a, u, h_in, reverse):
    ts = a.shape[0]
    sub = lax.broadcasted_iota(I32, a.shape, 0) & (SUBLANES - 1)
    k = 1
    while k < SUBLANES:
        if reverse:
            a_s, u_s, valid = pltpu.roll(a, ts - k, 0), pltpu.roll(u, ts - k, 0), sub < SUBLANES - k
        else:
            a_s, u_s, valid = pltpu.roll(a, k, 0), pltpu.roll(u, k, 0), sub >= k
        u = u + a * jnp.where(valid, u_s, 0.0)
        a = a * jnp.where(valid, a_s, 1.0)
        k *= 2
    n_group = ts // SUBLANES
    outs = [None] * n_group
    h = h_in
    for g in (range(n_group - 1, -1, -1) if reverse else range(n_group)):
        rows = slice(g * SUBLANES, (g + 1) * SUBLANES)
        outs[g] = u[rows] + a[rows] * h
        h = outs[g][0:1] if reverse else outs[g][SUBLANES - 1:SUBLANES]
    return jnp.concatenate(outs, axis=0), h


def _lru_kernel(*refs, reverse, final):
    if final:
        (x_ref, prev_ref, next_ref, cw_ref, cb_ref, wa_ref, ba_ref, wx_ref, bx_ref, sp_ref,
         gate_ref, hf_ref, o_ref, carry_ref) = refs
    else:
        (x_ref, prev_ref, next_ref, cw_ref, cb_ref, wa_ref, ba_ref, wx_ref, bx_ref, sp_ref,
         o_ref, carry_ref) = refs
    step, nstep = pl.program_id(1), pl.num_programs(1)
    blk = (nstep - 1 - step) if reverse else step

    @pl.when(step == 0)
    def _():
        carry_ref[...] = jnp.zeros_like(carry_ref)

    x = x_ref[...]
    ts = x.shape[0]
    rowi = lax.broadcasted_iota(I32, x.shape, 0)
    has_prev = jnp.where(blk > 0, 1.0, 0.0)
    has_next = jnp.where(blk < nstep - 1, 1.0, 0.0)
    p6, p7, n0 = prev_ref[6:7, :] * has_prev, prev_ref[7:8, :] * has_prev, next_ref[0:1, :] * has_next
    xm1 = jnp.where(rowi == 0, p7, pltpu.roll(x, 1, 0))
    xm2 = jnp.where(rowi == 0, p6, jnp.where(rowi == 1, p7, pltpu.roll(x, 2, 0)))
    xp1 = jnp.where(rowi == ts - 1, n0, pltpu.roll(x, ts - 1, 0))
    cw = cw_ref[...]
    xc = xm2 * cw[0:1, :] + xm1 * cw[1:2, :] + x * cw[2:3, :] + xp1 * cw[3:4, :] + cb_ref[...]
    xcb = xc.astype(BF16)

    def block_diag(w_ref, b_ref):
        outs = [jnp.dot(xcb[:, n * LRU_BW:(n + 1) * LRU_BW], w_ref[n], preferred_element_type=F32)
                for n in range(LRU_BLOCKS)]
        return jnp.concatenate(outs, axis=1) + b_ref[...]

    r = _sigmoid(block_diag(wa_ref, ba_ref))
    i = _sigmoid(block_diag(wx_ref, bx_ref))
    log_a = (-LRU_C) * r * sp_ref[...]
    a = jnp.exp(log_a)
    u = jnp.sqrt(1.0 - a * a) * (i * xc)
    h, carry_ref[...] = _scan_rows(a, u, carry_ref[...], reverse)
    if final:
        g = gate_ref[...]
        gelu = 0.5 * g * (1.0 + jnp.tanh(0.7978845608028654 * (g + 0.044715 * (g * g * g))))
        o_ref[...] = ((h + hf_ref[...]) * gelu).astype(o_ref.dtype)
    else:
        o_ref[...] = h


def lru_direction(proj, cw, cb, wa, ba, wx, bx, sp, reverse, ts, h_fwd=None):
    B, S, W2 = proj.shape
    W = W2 // 2
    nblk = S // ts
    final = h_fwd is not None
    sub = ts // 8

    def pos(c):
        return (nblk - 1 - c) if reverse else c

    tile = (None, ts, W)
    halo = (None, 8, W)
    vec = lambda r: pl.BlockSpec((r, W), lambda b, c: (0, 0))
    wspec = pl.BlockSpec((LRU_BLOCKS, LRU_BW, LRU_BW), lambda b, c: (0, 0, 0))
    in_specs = [pl.BlockSpec(tile, lambda b, c: (b, pos(c), 1)),
                pl.BlockSpec(halo, lambda b, c: (b, jnp.maximum(pos(c) * sub - 1, 0), 1)),
                pl.BlockSpec(halo, lambda b, c: (b, jnp.minimum((pos(c) + 1) * sub, S // 8 - 1), 1)),
                vec(4), vec(1), wspec, vec(1), wspec, vec(1), vec(1)]
    args = [proj, proj, proj, cw, cb, wa, ba, wx, bx, sp]
    if final:
        in_specs += [pl.BlockSpec(tile, lambda b, c: (b, pos(c), 0)), pl.BlockSpec(tile, lambda b, c: (b, pos(c), 0))]
        args += [proj, h_fwd]
    return pl.pallas_call(
        functools.partial(_lru_kernel, reverse=reverse, final=final),
        grid=(B, nblk), in_specs=in_specs, out_specs=pl.BlockSpec(tile, lambda b, c: (b, pos(c), 0)),
        out_shape=jax.ShapeDtypeStruct((B, S, W), BF16 if final else F32),
        scratch_shapes=[pltpu.VMEM((1, W), F32)],
        compiler_params=_cparams(("parallel", "arbitrary"), 48),
        name="lru_bwd" if reverse else "lru_fwd",
    )(*args)


def _router_kernel(x_ref, g_ref, wr_ref, aff_ref, ext_ref):
    x = x_ref[...]
    d = x.shape[1]
    xn = x * _rms_scale(x, d) * g_ref[...]
    logits = lax.dot_general(wr_ref[...], xn, NT_DIMS, precision=HIGHEST, preferred_element_type=F32)
    e = jnp.exp(logits - jnp.max(logits, axis=0, keepdims=True))
    aff = e / jnp.sum(e, axis=0, keepdims=True)
    aff_ref[...] = aff
    ext_ref[:, :d] = x
    ext_ref[:, d:] = jnp.concatenate([aff, jnp.zeros((LANES - aff.shape[0], aff.shape[1]), F32)], axis=0).T


def router(x, g, wr_t, tm):
    N, D = x.shape
    E = wr_t.shape[0]
    return pl.pallas_call(
        _router_kernel, grid=(N // tm,),
        in_specs=[pl.BlockSpec((tm, D), lambda i: (i, 0)), pl.BlockSpec((1, D), lambda i: (0, 0)),
                  pl.BlockSpec((E, D), lambda i: (0, 0))],
        out_specs=[pl.BlockSpec((E, tm), lambda i: (0, i)), pl.BlockSpec((tm, D + LANES), lambda i: (i, 0))],
        out_shape=[jax.ShapeDtypeStruct((E, N), F32), jax.ShapeDtypeStruct((N, D + LANES), F32)],
        compiler_params=_cparams(("parallel",), 32), name="moe_router",
    )(x, g.reshape(1, D), wr_t)


def _topk_kernel(aff_ref, pos_ref, off_ref, cnt_ref, *, cap):
    bits = pltpu.bitcast(aff_ref[...], I32)
    nb = bits.shape[0]

    def count(pred):
        return jnp.sum(jnp.sum(jnp.where(pred, 1.0, 0.0), axis=1, keepdims=True), axis=0, keepdims=True)

    def bisect(i, lo):
        cand = lo | lax.shift_left(jnp.int32(1), jnp.int32(30) - i)
        return jnp.where(count(bits >= cand) >= cap, cand, lo)

    thr = lax.fori_loop(0, 31, bisect, jnp.zeros((1, 1), I32))

    r128 = lax.broadcasted_iota(I32, (LANES, LANES), 0)
    c128 = lax.broadcasted_iota(I32, (LANES, LANES), 1)
    upper = jnp.where(r128 <= c128, 1.0, 0.0).astype(BF16)
    rn = lax.broadcasted_iota(I32, (nb, nb), 0)
    cn = lax.broadcasted_iota(I32, (nb, nb), 1)
    strict_lower = jnp.where(cn < rn, 1.0, 0.0).astype(BF16)

    def running_count(m):
        incl = jnp.dot(m.astype(BF16), upper, preferred_element_type=F32)
        tot = jnp.broadcast_to(incl[:, LANES - 1:LANES], (nb, LANES)).astype(BF16)
        return incl, jnp.dot(strict_lower, tot, preferred_element_type=F32)

    gt = jnp.where(bits > thr, 1.0, 0.0)
    eq = jnp.where(bits == thr, 1.0, 0.0)
    need = cap - count(bits > thr)
    eq_incl, eq_off = running_count(eq)
    sel = gt + eq * jnp.where(eq_incl - eq + eq_off < need, 1.0, 0.0)
    incl, off = running_count(sel)
    cnt = incl + off
    pos_ref[...] = jnp.where(sel > 0.0, cnt - 1.0, -1.0).astype(I32)
    off_ref[...] = off.astype(I32)
    cnt_ref[...] = cnt


def topk_slots(aff3, cap):
    E, NB, _ = aff3.shape
    spec = pl.BlockSpec((None, NB, LANES), lambda e: (e, 0, 0))
    return pl.pallas_call(
        functools.partial(_topk_kernel, cap=cap), grid=(E,), in_specs=[spec], out_specs=[spec, spec, spec],
        out_shape=[jax.ShapeDtypeStruct((E, NB, LANES), I32), jax.ShapeDtypeStruct((E, NB, LANES), I32),
                   jax.ShapeDtypeStruct((E, NB, LANES), F32)],
        compiler_params=_cparams(("parallel",), 32), name="moe_topk",
    )(aff3)


def _compact_kernel(blo_ref, bhi_ref, cnt_ref, idx_ref, *, n_chunk):
    e = pl.program_id(0)
    slot = lax.broadcasted_iota(I32, (LANES, LANES), 0)

    def chunk(c, carry):
        lo, hi = blo_ref[e, c], bhi_ref[e, c]
        sigma = (slot + c * LANES).astype(F32)

        def blk(b, acc):
            return acc + jnp.where(cnt_ref[b] <= sigma, 1.0, 0.0)

        acc = lax.fori_loop(lo, hi + 1, blk, jnp.zeros((LANES, LANES), F32))
        col = jnp.sum(acc, axis=1, keepdims=True) + (lo * LANES).astype(F32)
        idx_ref[c] = jnp.broadcast_to(col, (LANES, LANES)).T[0:1, :].astype(I32)
        return carry

    lax.fori_loop(0, n_chunk, chunk, 0)


def compact_slots(cnt3, cap):
    E, NB, _ = cnt3.shape
    n_chunk = cap // LANES
    starts = jnp.arange(n_chunk, dtype=F32) * LANES
    blo = jnp.sum(cnt3[:, :, LANES - 1, None] <= starts[None, None, :], axis=1).astype(I32)
    bhi = jnp.sum(cnt3[:, :, 0, None] <= starts[None, None, :] + (LANES - 1), axis=1).astype(I32) - 1
    grid_spec = pltpu.PrefetchScalarGridSpec(
        num_scalar_prefetch=2, grid=(E,),
        in_specs=[pl.BlockSpec((None, NB, 1, LANES), lambda e, *_: (e, 0, 0, 0))],
        out_specs=pl.BlockSpec((None, n_chunk, 1, LANES), lambda e, *_: (e, 0, 0, 0)))
    idx = pl.pallas_call(
        functools.partial(_compact_kernel, n_chunk=n_chunk), grid_spec=grid_spec,
        out_shape=jax.ShapeDtypeStruct((E, n_chunk, 1, LANES), I32),
        compiler_params=_cparams(("arbitrary",), 32), name="moe_compact",
    )(blo, bhi, cnt3.reshape(E, NB, 1, LANES))
    return idx.reshape(E, cap)


GATHER_UNROLL = 8


def _ffn_kernel(idx_ref, next_idx_ref, ext_hbm, g_ref, wg_ref, wu_ref, wd_ref, o_ref, rows_ref, sem,
                *, tiles_per_expert):
    t, n_tiles = pl.program_id(0), pl.num_programs(0)
    tm, d = o_ref.shape
    slot = t % 2

    def gather(ids_ref, s):
        def issue(i, carry):
            for j in range(GATHER_UNROLL):
                r = i * GATHER_UNROLL + j
                pltpu.make_async_copy(ext_hbm.at[pl.ds(ids_ref[0, 0, r], 1)], rows_ref.at[s, pl.ds(r, 1)],
                                      sem.at[s]).start()
            return carry

        lax.fori_loop(0, tm // GATHER_UNROLL, issue, 0)

    @pl.when(t == 0)
    def _():
        gather(idx_ref, 0)

    @pl.when(t + 1 < n_tiles)
    def _():
        gather(next_idx_ref, 1 - slot)

    pltpu.make_async_copy(ext_hbm.at[pl.ds(0, tm)], rows_ref.at[slot], sem.at[slot]).wait()
    ext = rows_ref[slot]
    x = ext[:, :d]
    lane = lax.broadcasted_iota(I32, (tm, LANES), 1)
    route_gate = jnp.sum(jnp.where(lane == t // tiles_per_expert, ext[:, d:], 0.0), axis=1, keepdims=True)
    xn = (x * _rms_scale(x, d) * g_ref[...]).astype(BF16)
    gate = jnp.dot(xn, wg_ref[...], preferred_element_type=F32)
    up = jnp.dot(xn, wu_ref[...], preferred_element_type=F32)
    hid = (gate * _sigmoid(gate) * up).astype(BF16)
    o_ref[...] = (jnp.dot(hid, wd_ref[...], preferred_element_type=F32) * route_gate).astype(o_ref.dtype)


def expert_ffn(idx, ext, g, wg, wu, wd, tm):
    E, cap = idx.shape
    D, FF = wg.shape[1], wg.shape[2]
    nc = cap // tm
    n_tiles = E * nc
    smem = lambda imap: pl.BlockSpec((1, 1, tm), imap, memory_space=pltpu.SMEM)
    ye = pl.pallas_call(
        functools.partial(_ffn_kernel, tiles_per_expert=nc), grid=(n_tiles,),
        in_specs=[smem(lambda t: (t, 0, 0)), smem(lambda t: (jnp.minimum(t + 1, n_tiles - 1), 0, 0)),
                  pl.BlockSpec(memory_space=pl.ANY),
                  pl.BlockSpec((1, D), lambda t: (0, 0)),
                  pl.BlockSpec((None, D, FF), lambda t: (t // nc, 0, 0)),
                  pl.BlockSpec((None, D, FF), lambda t: (t // nc, 0, 0)),
                  pl.BlockSpec((None, FF, D), lambda t: (t // nc, 0, 0))],
        out_specs=pl.BlockSpec((tm, D), lambda t: (t, 0)),
        out_shape=jax.ShapeDtypeStruct((E * cap, D), BF16),
        scratch_shapes=[pltpu.VMEM((2, tm, D + LANES), F32), pltpu.SemaphoreType.DMA((2,))],
        compiler_params=_cparams(("arbitrary",), 56), name="moe_ffn",
    )(idx.reshape(n_tiles, 1, tm), idx.reshape(n_tiles, 1, tm), ext, g.reshape(1, D), wg, wu, wd)
    return ye.reshape(E, cap, D)


def _combine_kernel(s0_ref, x_ref, pos_ref, ye_hbm, o_ref, buf_ref, sem, *, win):
    t, n_tiles = pl.program_id(0), pl.num_programs(0)
    n_exp = pos_ref.shape[1]
    par = t % 2

    def window(tile, e, p):
        s0 = pl.multiple_of(s0_ref[e, tile], 16)
        return pltpu.make_async_copy(ye_hbm.at[e, pl.ds(s0, win)], buf_ref.at[p, e], sem.at[p, e])

    @pl.when(t == 0)
    def _():
        for e in range(n_exp):
            window(0, e, 0).start()

    @pl.when(t + 1 < n_tiles)
    def _():
        for e in range(n_exp):
            window(t + 1, e, 1 - par).start()

    acc = x_ref[...]
    lane = lax.broadcasted_iota(I32, (acc.shape[0], win), 1)
    for e in range(n_exp):
        window(t, e, par).wait()
        onehot = jnp.where(pos_ref[:, e:e + 1] == lane + s0_ref[e, t], 1.0, 0.0).astype(BF16)
        acc = acc + jnp.dot(onehot, buf_ref[par, e], preferred_element_type=F32)
    o_ref[...] = acc


def moe_combine(x, pos_t, off, ye, tt):
    N, D = x.shape
    E, cap, _ = ye.shape
    win = tt + 16
    s0 = jnp.clip((off // 16) * 16, 0, cap - win).astype(I32)
    grid_spec = pltpu.PrefetchScalarGridSpec(
        num_scalar_prefetch=1, grid=(N // tt,),
        in_specs=[pl.BlockSpec((tt, D), lambda t, *_: (t, 0)), pl.BlockSpec((tt, E), lambda t, *_: (t, 0)),
                  pl.BlockSpec(memory_space=pl.ANY)],
        out_specs=pl.BlockSpec((tt, D), lambda t, *_: (t, 0)),
        scratch_shapes=[pltpu.VMEM((2, E, win, D), BF16), pltpu.SemaphoreType.DMA((2, E))])
    return pl.pallas_call(
        functools.partial(_combine_kernel, win=win), grid_spec=grid_spec,
        out_shape=jax.ShapeDtypeStruct((N, D), F32),
        compiler_params=_cparams(("arbitrary",), 40), name="moe_combine",
    )(s0, x, pos_t, ye)


def expert_choice_moe(x, g, wr_t, wg, wu, wd):
    N, D = x.shape
    E = wr_t.shape[0]
    cap = CAPACITY * N // E
    nb = N // LANES
    aff, ext = router(x, g, wr_t, 512)
    pos3, off3, cnt3 = topk_slots(aff.reshape(E, nb, LANES), cap)
    idx = compact_slots(cnt3, cap)
    ye = expert_ffn(idx, ext, g, wg, wu, wd, 512)
    return moe_combine(x, pos3.reshape(E, N).T, off3[:, :, 0], ye, LANES)


def _rope_tables(seq):
    pos = jnp.arange(seq, dtype=F32)
    inv = 1.0 / (ROPE_THETA ** (jnp.arange(0, MLA_ROPE, 2, dtype=F32) / MLA_ROPE))
    ang = pos[:, None] * inv[None, :]
    cos, sin = jnp.cos(ang), jnp.sin(ang)
    z32, z64 = jnp.zeros_like(cos), jnp.zeros((seq, LANES - MLA_ROPE), F32)
    return (jnp.concatenate([cos, cos, z64], axis=1), jnp.concatenate([-sin, z32, z64], axis=1),
            jnp.concatenate([z32, sin, z64], axis=1))


def _pad_cols(a, width):
    return jnp.pad(a, [(0, 0)] * (a.ndim - 1) + [(0, width - a.shape[-1])])


def even_layer(x, batch, seq, g_mix, w_in, w_out, lb, hg_gain, qn_g, kvn_g, w_uq, w_ukv, q_gain, k_gain):
    T, D = x.shape
    hgw = 5 * HG_WIDTH
    w_hg = w_in[:, :hgw].astype(BF16)
    w_mla = _pad_cols(w_in[:, hgw:], 6 * LANES).astype(BF16)
    proj_hg, proj_mla = norm_matmul(x, g_mix, [w_hg, w_mla], [F32, F32], 256)

    proj3 = proj_hg.reshape(batch, seq, hgw)
    rows = 256
    o_f = hgrn_direction(proj3, lb[0:1], False, rows)
    o_hg = hgrn_direction(proj3, lb[1:2], True, rows, o_fwd=o_f, gain=hg_gain)

    wq = _pad_cols(w_uq.reshape(MLA_Q_RANK, MLA_HEADS, MLA_QK), MLA_QK_PAD).reshape(MLA_Q_RANK, -1).astype(BF16)
    wkv = w_ukv.astype(BF16)
    qg = _pad_cols(q_gain.reshape(1, MLA_QK), MLA_QK_PAD)
    kg = _pad_cols(k_gain.reshape(1, MLA_QK), MLA_QK_PAD)
    q, k, v = mla_prep(proj_mla, qn_g.reshape(1, -1), kvn_g.reshape(1, -1), wq, wkv, qg, kg,
                       _rope_tables(seq), seq, 512)
    hq = MLA_HEADS * MLA_QK_PAD
    tq = tk = min(seq, 1024)
    o_mla = attention(q.reshape(batch, seq, hq), k.reshape(batch, seq, hq),
                      v.reshape(batch, seq, hq), tq, tk)

    w_o = w_out.astype(BF16)
    return matmul_resid(x, [o_hg.reshape(T, HG_WIDTH), o_mla.reshape(T, MLA_HEADS * MLA_V)],
                        [w_o[:HG_WIDTH], w_o[HG_WIDTH:]], 512)


def odd_layer(x, batch, seq, g_mix, w_in, conv_w, conv_b, wa, ba, wx, bx, lam, w_out):
    T, D = x.shape
    (proj,) = norm_matmul(x, g_mix, [w_in.astype(BF16)], [F32], 256)
    proj3 = proj.reshape(batch, seq, 2 * D)
    sp = jax.nn.softplus(-lam.astype(F32))
    cb = conv_b.reshape(1, D)
    ts = 128
    h_f = lru_direction(proj3, conv_w, cb, wa[0].astype(BF16), ba[0:1], wx[0].astype(BF16), bx[0:1], sp[0:1],
                        False, ts)
    y = lru_direction(proj3, conv_w, cb, wa[1].astype(BF16), ba[1:2], wx[1].astype(BF16), bx[1:2], sp[1:2],
                      True, ts, h_fwd=h_f)
    return matmul_resid(x, [y.reshape(T, D)], [w_out.astype(BF16)], 512)


def _trunk(x3, norm_mix, norm_ffn, ev_w_in, ev_w_out, hg_lb_logits, hg_out_gain, mla_q_norm, mla_kv_norm,
           mla_w_uq, mla_w_ukv, mla_q_gain, mla_k_gain, od_w_in, od_conv_w, od_conv_b, rg_w_a, rg_b_a,
           rg_w_x, rg_b_x, rg_lambda, od_w_out, moe_router, moe_w_gate, moe_w_up, moe_w_down):
    batch, seq, D = x3.shape
    x = x3.reshape(batch * seq, D)
    lb_all = jnp.cumsum(jax.nn.softmax(hg_lb_logits.astype(F32), axis=1), axis=1)
    depth = norm_mix.shape[0]
    for layer in range(depth):
        j = layer // 2
        if layer % 2 == 0:
            x = even_layer(x, batch, seq, norm_mix[layer], ev_w_in[j], ev_w_out[j], lb_all[:, j],
                           hg_out_gain[j], mla_q_norm[j], mla_kv_norm[j], mla_w_uq[j], mla_w_ukv[j],
                           mla_q_gain[j], mla_k_gain[j])
        else:
            x = odd_layer(x, batch, seq, norm_mix[layer], od_w_in[j], od_conv_w[j], od_conv_b[j], rg_w_a[j],
                          rg_b_a[j], rg_w_x[j], rg_b_x[j], rg_lambda[j], od_w_out[j])
        x = expert_choice_moe(x, norm_ffn[layer], moe_router[layer].T, moe_w_gate[layer].astype(BF16),
                              moe_w_up[layer].astype(BF16), moe_w_down[layer].astype(BF16))
    return x.reshape(batch, seq, D)


def kernel(x_prompt, x_sample, norm_mix, norm_ffn, ev_w_in, ev_w_out, hg_lb_logits, hg_out_gain, mla_q_norm,
           mla_kv_norm, mla_w_uq, mla_w_ukv, mla_q_gain, mla_k_gain, od_w_in, od_conv_w, od_conv_b, rg_w_a,
           rg_b_a, rg_w_x, rg_b_x, rg_lambda, od_w_out, moe_router, moe_w_gate, moe_w_up, moe_w_down):
    params = (norm_mix, norm_ffn, ev_w_in, ev_w_out, hg_lb_logits, hg_out_gain, mla_q_norm, mla_kv_norm,
              mla_w_uq, mla_w_ukv, mla_q_gain, mla_k_gain, od_w_in, od_conv_w, od_conv_b, rg_w_a, rg_b_a,
              rg_w_x, rg_b_x, rg_lambda, od_w_out, moe_router, moe_w_gate, moe_w_up, moe_w_down)
    return (_trunk(x_prompt, *params), _trunk(x_sample, *params))
```

```python
import functools

import jax
import jax.numpy as jnp
from jax import lax
from jax.experimental import pallas as pl
from jax.experimental.pallas import tpu as pltpu

F32 = jnp.float32
BF16 = jnp.bfloat16
I32 = jnp.int32
EPS = 1e-6
HIGHEST = lax.Precision.HIGHEST
NT_DIMS = (((1,), (1,)), ((), ()))
TN_DIMS = (((0,), (0,)), ((), ()))

D_MODEL = 1024
HG_HEADS = 4
HG_DK = 128
HG_WIDTH = 512
HG_CHUNK = 64
HG_SUB = 16
HG_EXP_CLAMP = 80.0
MLA_HEADS = 4
MLA_Q_RANK = 384
MLA_KV_RANK = 256
MLA_NOPE = 128
MLA_ROPE = 64
MLA_V = 128
MLA_QK = MLA_NOPE + MLA_ROPE
MLA_QK_PAD = 256
ROPE_THETA = 10000.0
LRU_BLOCKS = 4
LRU_BW = 256
LRU_C = 8.0
N_EXPERTS = 16
CAPACITY = 2
LANES = 128
LOG2_E = 1.4426950408889634
MIB = 1024 * 1024


def _cparams(semantics, vmem_mib):
    return pltpu.CompilerParams(dimension_semantics=semantics, vmem_limit_bytes=vmem_mib * MIB)


def _sigmoid(x):
    return 1.0 / (1.0 + jnp.exp(-x))


def _rms_scale(x, width):
    return lax.rsqrt(jnp.sum(x * x, axis=-1, keepdims=True) * (1.0 / width) + EPS)


def _norm_matmul_kernel(x_ref, g_ref, *refs, n_out):
    x = x_ref[...]
    h = (x * _rms_scale(x, x.shape[-1]) * g_ref[...]).astype(BF16)
    for w_ref, o_ref in zip(refs[:n_out], refs[n_out:]):
        o_ref[...] = jnp.dot(h, w_ref[...], preferred_element_type=F32).astype(o_ref.dtype)


def norm_matmul(x, g, ws, out_dtypes, tm):
    T, D = x.shape
    n = len(ws)
    in_specs = [pl.BlockSpec((tm, D), lambda i: (i, 0)), pl.BlockSpec((1, D), lambda i: (0, 0))]
    in_specs += [pl.BlockSpec(w.shape, lambda i: (0, 0)) for w in ws]
    out_specs = [pl.BlockSpec((tm, w.shape[1]), lambda i: (i, 0)) for w in ws]
    out_shape = [jax.ShapeDtypeStruct((T, w.shape[1]), dt) for w, dt in zip(ws, out_dtypes)]
    return pl.pallas_call(
        functools.partial(_norm_matmul_kernel, n_out=n),
        grid=(T // tm,), in_specs=in_specs, out_specs=out_specs, out_shape=out_shape,
        compiler_params=_cparams(("parallel",), 56), name="norm_matmul",
    )(x, g.reshape(1, D), *ws)


def _matmul_resid_kernel(r_ref, *refs, n_in):
    acc = r_ref[...]
    for a_ref, w_ref in zip(refs[:n_in], refs[n_in:2 * n_in]):
        acc = acc + jnp.dot(a_ref[...], w_ref[...], preferred_element_type=F32)
    refs[2 * n_in][...] = acc


def matmul_resid(resid, acts, ws, tm):
    T, D = resid.shape
    n = len(acts)
    in_specs = [pl.BlockSpec((tm, D), lambda i: (i, 0))]
    in_specs += [pl.BlockSpec((tm, a.shape[1]), lambda i: (i, 0)) for a in acts]
    in_specs += [pl.BlockSpec(w.shape, lambda i: (0, 0)) for w in ws]
    return pl.pallas_call(
        functools.partial(_matmul_resid_kernel, n_in=n),
        grid=(T // tm,), in_specs=in_specs, out_specs=pl.BlockSpec((tm, D), lambda i: (i, 0)),
        out_shape=jax.ShapeDtypeStruct((T, D), F32),
        compiler_params=_cparams(("parallel",), 40), name="matmul_resid",
    )(resid, *acts, *ws)


def _hgrn_kernel(*refs, reverse, final, n_chunk):
    if final:
        q_ref, v_ref, fz_ref, lb_ref, g_ref, of_ref, gain_ref, o_ref, st_ref = refs
    else:
        q_ref, v_ref, fz_ref, lb_ref, o_ref, st_ref = refs
    C, SB = HG_CHUNK, HG_SUB

    @pl.when(pl.program_id(1) == 0)
    def _():
        st_ref[...] = jnp.zeros_like(st_ref)

    row = lax.broadcasted_iota(I32, (C, C), 0)
    col = lax.broadcasted_iota(I32, (C, C), 1)
    blk_of_row = row // SB
    if reverse:
        tri = col >= row
        ref_of_row = blk_of_row * SB
        causal = col >= row
    else:
        tri = col <= row
        ref_of_row = blk_of_row * SB + (SB - 1)
        causal = col <= row
    ref_sum = (col >= ref_of_row) if reverse else (col <= ref_of_row)
    cum_mat = jnp.concatenate([jnp.where(tri, 1.0, 0.0), jnp.where(ref_sum, 1.0, 0.0)], axis=0)
    row_blk = lax.broadcasted_iota(I32, (C, HG_WIDTH), 0) // SB

    heads = [slice(h * HG_DK, (h + 1) * HG_DK) for h in range(HG_HEADS)]
    order = [((n_chunk - 1 - ci) if reverse else ci) for ci in range(n_chunk)]
    lb = lb_ref[...]

    per_chunk = []
    for c in order:
        rows = slice(c * C, (c + 1) * C)
        q = q_ref[rows, :] * (HG_DK ** -0.5)
        f = lb + (1.0 - lb) * _sigmoid(fz_ref[rows, :])
        logf = jnp.log(f)
        k = 1.0 - f
        bb = jnp.dot(cum_mat, logf, precision=HIGHEST, preferred_element_type=F32)
        b, b_blk = bb[:C], bb[C:]
        kt = k * jnp.exp(b_blk - b)
        q_parts, k_parts = [], []
        for j in range(C // SB):
            jr = j * SB if reverse else j * SB + SB - 1
            q_parts.append((q * jnp.exp(jnp.minimum(b - b[jr:jr + 1, :], HG_EXP_CLAMP))).astype(BF16))
            k_parts.append(jnp.where(row_blk == j, kt, 0.0).astype(BF16))
        b_end = b[0:1, :] if reverse else b[C - 1:C, :]
        vb = v_ref[rows, :].astype(BF16)
        intra = []
        for cs in heads:
            scores = lax.dot_general(jnp.concatenate([p[:, cs] for p in q_parts], axis=1),
                                     jnp.concatenate([p[:, cs] for p in k_parts], axis=1),
                                     NT_DIMS, preferred_element_type=F32)
            scores = jnp.where(causal, scores, 0.0).astype(BF16)
            intra.append(jnp.dot(scores, vb[:, cs], preferred_element_type=F32))
        per_chunk.append((rows, intra, (q * jnp.exp(b)).astype(BF16), (k * jnp.exp(b_end - b)).astype(BF16),
                          jnp.exp(b_end), vb))

    states = [st_ref[h] for h in range(HG_HEADS)]
    for rows, intra, q_dec, k_hat, decay_end, vb in per_chunk:
        for h, cs in enumerate(heads):
            o = intra[h] + lax.dot_general(q_dec[:, cs], states[h].astype(BF16), NT_DIMS,
                                           preferred_element_type=F32)
            states[h] = states[h] * decay_end[:, cs] + lax.dot_general(
                vb[:, cs], k_hat[:, cs], TN_DIMS, preferred_element_type=F32)
            if final:
                o = o + of_ref[rows, cs]
                y = o * _rms_scale(o, HG_DK) * gain_ref[...]
                g = g_ref[rows, cs]
                o_ref[rows, cs] = (y * (g * _sigmoid(g))).astype(o_ref.dtype)
            else:
                o_ref[rows, cs] = o
    for h in range(HG_HEADS):
        st_ref[h] = states[h]


def hgrn_direction(proj, lb, reverse, rows, o_fwd=None, gain=None):
    B, S, _ = proj.shape
    W = HG_WIDTH
    nblk = S // rows
    final = o_fwd is not None

    def rowmap(colblk):
        if reverse:
            return lambda b, c: (b, nblk - 1 - c, colblk)
        return lambda b, c: (b, c, colblk)

    blk = (None, rows, W)
    in_specs = [pl.BlockSpec(blk, rowmap(0)), pl.BlockSpec(blk, rowmap(1)),
                pl.BlockSpec(blk, rowmap(4 if reverse else 3)), pl.BlockSpec((1, W), lambda b, c: (0, 0))]
    args = [proj, proj, proj, lb]
    if final:
        in_specs += [pl.BlockSpec(blk, rowmap(2)), pl.BlockSpec(blk, rowmap(0)),
                     pl.BlockSpec((1, HG_DK), lambda b, c: (0, 0))]
        args += [proj, o_fwd, gain.reshape(1, HG_DK)]
    return pl.pallas_call(
        functools.partial(_hgrn_kernel, reverse=reverse, final=final, n_chunk=rows // HG_CHUNK),
        grid=(B, nblk), in_specs=in_specs, out_specs=pl.BlockSpec(blk, rowmap(0)),
        out_shape=jax.ShapeDtypeStruct((B, S, W), BF16 if final else F32),
        scratch_shapes=[pltpu.VMEM((HG_HEADS, HG_DK, HG_DK), F32)],
        compiler_params=_cparams(("parallel", "arbitrary"), 40),
        name="hgrn_bwd" if reverse else "hgrn_fwd",
    )(*args)


def _rope(x, cos_t, sin_a, sin_b):
    return x * cos_t + pltpu.roll(x, LANES - MLA_ROPE // 2, 1) * sin_a + pltpu.roll(x, MLA_ROPE // 2, 1) * sin_b


def _mla_prep_kernel(c_ref, qn_ref, kvn_ref, wq_ref, wkv_ref, qg_ref, kg_ref, cos_ref, sa_ref, sb_ref,
                     q_ref, k_ref, v_ref):
    c = c_ref[...]
    cq = c[:, :MLA_Q_RANK]
    ckv = c[:, MLA_Q_RANK:MLA_Q_RANK + MLA_KV_RANK]
    kpe = c[:, MLA_Q_RANK + MLA_KV_RANK:]
    qn = (cq * _rms_scale(cq, MLA_Q_RANK) * qn_ref[...]).astype(BF16)
    kvn = (ckv * _rms_scale(ckv, MLA_KV_RANK) * kvn_ref[...]).astype(BF16)
    qf = jnp.dot(qn, wq_ref[...], preferred_element_type=F32)
    kvf = jnp.dot(kvn, wkv_ref[...], preferred_element_type=F32)
    cos_t, sin_a, sin_b = cos_ref[...], sa_ref[...], sb_ref[...]
    qg, kg = qg_ref[...], kg_ref[...]
    sm_scale = (MLA_QK ** -0.5) * LOG2_E
    kpe_ss = jnp.sum(kpe * kpe, axis=-1, keepdims=True)
    for h in range(MLA_HEADS):
        lo = h * MLA_QK_PAD
        q_nope, q_rope = qf[:, lo:lo + LANES], qf[:, lo + LANES:lo + 2 * LANES]
        qs = lax.rsqrt((jnp.sum(q_nope * q_nope, axis=-1, keepdims=True)
                        + jnp.sum(q_rope * q_rope, axis=-1, keepdims=True)) * (1.0 / MLA_QK) + EPS)
        q_ref[:, lo:lo + LANES] = (q_nope * qs * qg[:, :LANES] * sm_scale).astype(BF16)
        q_ref[:, lo + LANES:lo + 2 * LANES] = (
            _rope(q_rope * qs * qg[:, LANES:], cos_t, sin_a, sin_b) * sm_scale).astype(BF16)
        k_nope = kvf[:, lo:lo + LANES]
        ks = lax.rsqrt((jnp.sum(k_nope * k_nope, axis=-1, keepdims=True) + kpe_ss) * (1.0 / MLA_QK) + EPS)
        k_ref[:, lo:lo + LANES] = (k_nope * ks * kg[:, :LANES]).astype(BF16)
        k_ref[:, lo + LANES:lo + 2 * LANES] = _rope(kpe * ks * kg[:, LANES:], cos_t, sin_a, sin_b).astype(BF16)
        v_ref[:, lo:lo + LANES] = kvf[:, lo + LANES:lo + 2 * LANES].astype(BF16)
        v_ref[:, lo + LANES:lo + 2 * LANES] = jnp.ones((c.shape[0], LANES), BF16)


def mla_prep(c, qn_g, kvn_g, wq, wkv, qg, kg, tables, seq, tm):
    T, CW = c.shape
    nseq = seq // tm
    full = lambda a: pl.BlockSpec(a.shape, lambda i: (0, 0))
    tab = pl.BlockSpec((tm, LANES), lambda i: (i % nseq, 0))
    HW = MLA_HEADS * MLA_QK_PAD
    return pl.pallas_call(
        _mla_prep_kernel, grid=(T // tm,),
        in_specs=[pl.BlockSpec((tm, CW), lambda i: (i, 0)), full(qn_g), full(kvn_g), full(wq), full(wkv),
                  full(qg), full(kg), tab, tab, tab],
        out_specs=[pl.BlockSpec((tm, HW), lambda i: (i, 0))] * 3,
        out_shape=[jax.ShapeDtypeStruct((T, HW), BF16)] * 3,
        compiler_params=_cparams(("parallel",), 40), name="mla_prep",
    )(c, qn_g, kvn_g, wq, wkv, qg, kg, *tables)


ATTN_PARTS = 4


def _attn_kernel(q_ref, k_ref, v_ref, o_ref, m_ref, l_ref, acc_ref):
    ki = pl.program_id(3)

    @pl.when(ki == 0)
    def _():
        m_ref[...] = jnp.full_like(m_ref, -jnp.inf)
        l_ref[...] = jnp.zeros_like(l_ref)
        acc_ref[...] = jnp.zeros_like(acc_ref)

    k, v = k_ref[...], v_ref[...]
    part = q_ref.shape[0] // ATTN_PARTS
    rows = [slice(i * part, (i + 1) * part) for i in range(ATTN_PARTS)]
    scores = lambda r: lax.dot_general(q_ref[r, :], k, NT_DIMS, preferred_element_type=F32)
    s_next = scores(rows[0])
    for i, r in enumerate(rows):
        s = s_next
        if i + 1 < ATTN_PARTS:
            s_next = scores(rows[i + 1])
        m_prev = m_ref[r, :]
        m_new = jnp.maximum(m_prev, jnp.max(s, axis=-1, keepdims=True))
        alpha = jnp.exp2(m_prev - m_new)
        p = jnp.exp2(s - m_new)
        l_ref[r, :] = alpha * l_ref[r, :] + jnp.sum(p, axis=-1, keepdims=True)
        acc_ref[r, :] = alpha * acc_ref[r, :] + jnp.dot(p.astype(BF16), v, preferred_element_type=F32)
        m_ref[r, :] = m_new

    @pl.when(ki == pl.num_programs(3) - 1)
    def _():
        o_ref[...] = (acc_ref[...] / l_ref[...]).astype(o_ref.dtype)


def attention(q, k, v, tq, tk):
    B, S, _ = q.shape
    return pl.pallas_call(
        _attn_kernel, grid=(B, MLA_HEADS, S // tq, S // tk),
        in_specs=[pl.BlockSpec((None, tq, MLA_QK_PAD), lambda b, h, i, j: (b, i, h)),
                  pl.BlockSpec((None, tk, MLA_QK_PAD), lambda b, h, i, j: (b, j, h)),
                  pl.BlockSpec((None, tk, MLA_V), lambda b, h, i, j: (b, j, 2 * h))],
        out_specs=pl.BlockSpec((None, tq, MLA_V), lambda b, h, i, j: (b, i, h)),
        out_shape=jax.ShapeDtypeStruct((B, S, MLA_HEADS * MLA_V), BF16),
        scratch_shapes=[pltpu.VMEM((tq, 1), F32), pltpu.VMEM((tq, 1), F32), pltpu.VMEM((tq, MLA_V), F32)],
        compiler_params=_cparams(("parallel", "parallel", "parallel", "arbitrary"), 48), name="mla_attention",
    )(q, k, v)


SUBLANES = 8


def _scan_rows(a, u, h_in, reverse):
    n_group = a.shape[0] // SUBLANES
    sub = lax.broadcasted_iota(I32, (SUBLANES, a.shape[1]), 0)
    scanned = []
    for g in range(n_group):
        rows = slice(g * SUBLANES, (g + 1) * SUBLANES)
        ag, ug = a[rows], u[rows]
        k = 1
        while k < SUBLANES:
            shift, valid = (SUBLANES - k, sub < SUBLANES - k) if reverse else (k, sub >= k)
            a_s, u_s = pltpu.roll(ag, shift, 0), pltpu.roll(ug, shift, 0)
            ug = ug + ag * jnp.where(valid, u_s, 0.0)
            ag = ag * jnp.where(valid, a_s, 1.0)
            k *= 2
        scanned.append((ag, ug))
    outs = [None] * n_group
    h = h_in
    for g in (range(n_group - 1, -1, -1) if reverse else range(n_group)):
        ag, ug = scanned[g]
        outs[g] = ug + ag * h
        h = outs[g][0:1] if reverse else outs[g][SUBLANES - 1:SUBLANES]
    return jnp.concatenate(outs, axis=0), h


def _lru_kernel(*refs, reverse, final):
    if final:
        (x_ref, prev_ref, next_ref, cw_ref, cb_ref, wa_ref, ba_ref, wx_ref, bx_ref, sp_ref,
         gate_ref, hf_ref, o_ref, carry_ref) = refs
    else:
        (x_ref, prev_ref, next_ref, cw_ref, cb_ref, wa_ref, ba_ref, wx_ref, bx_ref, sp_ref,
         o_ref, carry_ref) = refs
    step, nstep = pl.program_id(1), pl.num_programs(1)
    blk = (nstep - 1 - step) if reverse else step

    @pl.when(step == 0)
    def _():
        carry_ref[...] = jnp.zeros_like(carry_ref)

    x = x_ref[...]
    ts = x.shape[0]
    rowi = lax.broadcasted_iota(I32, x.shape, 0)
    has_prev = jnp.where(blk > 0, 1.0, 0.0)
    has_next = jnp.where(blk < nstep - 1, 1.0, 0.0)
    p6, p7, n0 = prev_ref[6:7, :] * has_prev, prev_ref[7:8, :] * has_prev, next_ref[0:1, :] * has_next
    xm1 = jnp.where(rowi == 0, p7, pltpu.roll(x, 1, 0))
    xm2 = jnp.where(rowi == 0, p6, jnp.where(rowi == 1, p7, pltpu.roll(x, 2, 0)))
    xp1 = jnp.where(rowi == ts - 1, n0, pltpu.roll(x, ts - 1, 0))
    cw = cw_ref[...]
    xc = xm2 * cw[0:1, :] + xm1 * cw[1:2, :] + x * cw[2:3, :] + xp1 * cw[3:4, :] + cb_ref[...]
    xcb = xc.astype(BF16)

    def block_diag(w_ref, b_ref):
        outs = [jnp.dot(xcb[:, n * LRU_BW:(n + 1) * LRU_BW], w_ref[n], preferred_element_type=F32)
                for n in range(LRU_BLOCKS)]
        return jnp.concatenate(outs, axis=1) + b_ref[...]

    r = _sigmoid(block_diag(wa_ref, ba_ref))
    i = _sigmoid(block_diag(wx_ref, bx_ref))
    log_a = (-LRU_C) * r * sp_ref[...]
    a = jnp.exp(log_a)
    u = jnp.sqrt(1.0 - a * a) * (i * xc)
    h, carry_ref[...] = _scan_rows(a, u, carry_ref[...], reverse)
    if final:
        g = gate_ref[...]
        gelu = 0.5 * g * (1.0 + jnp.tanh(0.7978845608028654 * (g + 0.044715 * (g * g * g))))
        o_ref[...] = ((h + hf_ref[...]) * gelu).astype(o_ref.dtype)
    else:
        o_ref[...] = h


def lru_direction(proj, cw, cb, wa, ba, wx, bx, sp, reverse, ts, h_fwd=None):
    B, S, W2 = proj.shape
    W = W2 // 2
    nblk = S // ts
    final = h_fwd is not None
    sub = ts // 8

    def pos(c):
        return (nblk - 1 - c) if reverse else c

    tile = (None, ts, W)
    halo = (None, 8, W)
    vec = lambda r: pl.BlockSpec((r, W), lambda b, c: (0, 0))
    wspec = pl.BlockSpec((LRU_BLOCKS, LRU_BW, LRU_BW), lambda b, c: (0, 0, 0))
    in_specs = [pl.BlockSpec(tile, lambda b, c: (b, pos(c), 1)),
                pl.BlockSpec(halo, lambda b, c: (b, jnp.maximum(pos(c) * sub - 1, 0), 1)),
                pl.BlockSpec(halo, lambda b, c: (b, jnp.minimum((pos(c) + 1) * sub, S // 8 - 1), 1)),
                vec(4), vec(1), wspec, vec(1), wspec, vec(1), vec(1)]
    args = [proj, proj, proj, cw, cb, wa, ba, wx, bx, sp]
    if final:
        in_specs += [pl.BlockSpec(tile, lambda b, c: (b, pos(c), 0)), pl.BlockSpec(tile, lambda b, c: (b, pos(c), 0))]
        args += [proj, h_fwd]
    return pl.pallas_call(
        functools.partial(_lru_kernel, reverse=reverse, final=final),
        grid=(B, nblk), in_specs=in_specs, out_specs=pl.BlockSpec(tile, lambda b, c: (b, pos(c), 0)),
        out_shape=jax.ShapeDtypeStruct((B, S, W), BF16 if final else F32),
        scratch_shapes=[pltpu.VMEM((1, W), F32)],
        compiler_params=_cparams(("parallel", "arbitrary"), 48),
        name="lru_bwd" if reverse else "lru_fwd",
    )(*args)


def _router_kernel(x_ref, g_ref, wr_ref, aff_ref, ext_ref):
    x = x_ref[...]
    d = x.shape[1]
    xn = x * _rms_scale(x, d) * g_ref[...]
    logits = lax.dot_general(wr_ref[...], xn, NT_DIMS, precision=HIGHEST, preferred_element_type=F32)
    e = jnp.exp(logits - jnp.max(logits, axis=0, keepdims=True))
    aff = e / jnp.sum(e, axis=0, keepdims=True)
    aff_ref[...] = aff
    ext_ref[:, :d] = x
    ext_ref[:, d:] = jnp.concatenate([aff, jnp.zeros((LANES - aff.shape[0], aff.shape[1]), F32)], axis=0).T


def router(x, g, wr_t, tm):
    N, D = x.shape
    E = wr_t.shape[0]
    return pl.pallas_call(
        _router_kernel, grid=(N // tm,),
        in_specs=[pl.BlockSpec((tm, D), lambda i: (i, 0)), pl.BlockSpec((1, D), lambda i: (0, 0)),
                  pl.BlockSpec((E, D), lambda i: (0, 0))],
        out_specs=[pl.BlockSpec((E, tm), lambda i: (0, i)), pl.BlockSpec((tm, D + LANES), lambda i: (i, 0))],
        out_shape=[jax.ShapeDtypeStruct((E, N), F32), jax.ShapeDtypeStruct((N, D + LANES), F32)],
        compiler_params=_cparams(("parallel",), 32), name="moe_router",
    )(x, g.reshape(1, D), wr_t)


def _topk_kernel(aff_ref, pos_ref, off_ref, cnt_ref, *, cap):
    bits = pltpu.bitcast(aff_ref[...], I32)
    nb = bits.shape[0]

    def count(pred):
        return jnp.sum(jnp.sum(jnp.where(pred, 1.0, 0.0), axis=1, keepdims=True), axis=0, keepdims=True)

    def bisect(i, lo):
        cand = lo | lax.shift_left(jnp.int32(1), jnp.int32(30) - i)
        return jnp.where(count(bits >= cand) >= cap, cand, lo)

    thr = lax.fori_loop(0, 31, bisect, jnp.zeros((1, 1), I32))

    r128 = lax.broadcasted_iota(I32, (LANES, LANES), 0)
    c128 = lax.broadcasted_iota(I32, (LANES, LANES), 1)
    upper = jnp.where(r128 <= c128, 1.0, 0.0).astype(BF16)
    rn = lax.broadcasted_iota(I32, (nb, nb), 0)
    cn = lax.broadcasted_iota(I32, (nb, nb), 1)
    strict_lower = jnp.where(cn < rn, 1.0, 0.0).astype(BF16)

    def running_count(m):
        incl = jnp.dot(m.astype(BF16), upper, preferred_element_type=F32)
        tot = jnp.broadcast_to(incl[:, LANES - 1:LANES], (nb, LANES)).astype(BF16)
        return incl, jnp.dot(strict_lower, tot, preferred_element_type=F32)

    gt = jnp.where(bits > thr, 1.0, 0.0)
    eq = jnp.where(bits == thr, 1.0, 0.0)
    need = cap - count(bits > thr)
    eq_incl, eq_off = running_count(eq)
    sel = gt + eq * jnp.where(eq_incl - eq + eq_off < need, 1.0, 0.0)
    incl, off = running_count(sel)
    cnt = incl + off
    pos_ref[...] = jnp.where(sel > 0.0, cnt - 1.0, -1.0).astype(I32)
    off_ref[...] = off.astype(I32)
    cnt_ref[...] = cnt


def topk_slots(aff3, cap):
    E, NB, _ = aff3.shape
    spec = pl.BlockSpec((None, NB, LANES), lambda e: (e, 0, 0))
    return pl.pallas_call(
        functools.partial(_topk_kernel, cap=cap), grid=(E,), in_specs=[spec], out_specs=[spec, spec, spec],
        out_shape=[jax.ShapeDtypeStruct((E, NB, LANES), I32), jax.ShapeDtypeStruct((E, NB, LANES), I32),
                   jax.ShapeDtypeStruct((E, NB, LANES), F32)],
        compiler_params=_cparams(("parallel",), 32), name="moe_topk",
    )(aff3)


def _compact_kernel(blo_ref, bhi_ref, cnt_ref, idx_ref, *, n_chunk):
    e = pl.program_id(0)
    slot = lax.broadcasted_iota(I32, (LANES, LANES), 0)

    def chunk(c, carry):
        lo, hi = blo_ref[e, c], bhi_ref[e, c]
        sigma = (slot + c * LANES).astype(F32)

        def blk(b, acc):
            return acc + jnp.where(cnt_ref[b] <= sigma, 1.0, 0.0)

        acc = lax.fori_loop(lo, hi + 1, blk, jnp.zeros((LANES, LANES), F32))
        col = jnp.sum(acc, axis=1, keepdims=True) + (lo * LANES).astype(F32)
        idx_ref[c] = jnp.broadcast_to(col, (LANES, LANES)).T[0:1, :].astype(I32)
        return carry

    lax.fori_loop(0, n_chunk, chunk, 0)


def compact_slots(cnt3, cap):
    E, NB, _ = cnt3.shape
    n_chunk = cap // LANES
    starts = jnp.arange(n_chunk, dtype=F32) * LANES
    blo = jnp.sum(cnt3[:, :, LANES - 1, None] <= starts[None, None, :], axis=1).astype(I32)
    bhi = jnp.sum(cnt3[:, :, 0, None] <= starts[None, None, :] + (LANES - 1), axis=1).astype(I32) - 1
    grid_spec = pltpu.PrefetchScalarGridSpec(
        num_scalar_prefetch=2, grid=(E,),
        in_specs=[pl.BlockSpec((None, NB, 1, LANES), lambda e, *_: (e, 0, 0, 0))],
        out_specs=pl.BlockSpec((None, n_chunk, 1, LANES), lambda e, *_: (e, 0, 0, 0)))
    idx = pl.pallas_call(
        functools.partial(_compact_kernel, n_chunk=n_chunk), grid_spec=grid_spec,
        out_shape=jax.ShapeDtypeStruct((E, n_chunk, 1, LANES), I32),
        compiler_params=_cparams(("arbitrary",), 32), name="moe_compact",
    )(blo, bhi, cnt3.reshape(E, NB, 1, LANES))
    return idx.reshape(E, cap)


def _ffn_kernel(idx0_ref, idx1_ref, idx2_ref, ext_hbm, g_ref, wg_ref, wu_ref, wd_ref, o_ref, rows_a, rows_b, sem,
                *, steps_per_expert):
    t, n_steps = pl.program_id(0), pl.num_programs(0)
    tm, d = rows_a.shape[0], o_ref.shape[1]

    def row_copy(ids_ref, rows_ref, s, r):
        return pltpu.make_async_copy(ext_hbm.at[pl.ds(ids_ref[0, 0, r], 1)], rows_ref.at[pl.ds(r, 1)], sem.at[s])

    def start_rows(ids_ref, rows_ref, s):
        for r in range(tm):
            row_copy(ids_ref, rows_ref, s, r).start()

    def wait_rows(rows_ref, s):
        pltpu.make_async_copy(ext_hbm.at[pl.ds(0, tm)], rows_ref, sem.at[s]).wait()

    def ffn(rows_ref, out_rows):
        ext = rows_ref[...]
        x = ext[:, :d]
        lane = lax.broadcasted_iota(I32, (tm, LANES), 1)
        route_gate = jnp.sum(jnp.where(lane == t // steps_per_expert, ext[:, d:], 0.0), axis=1, keepdims=True)
        xn = (x * _rms_scale(x, d) * g_ref[...]).astype(BF16)
        gate = jnp.dot(xn, wg_ref[...], preferred_element_type=F32)
        up = jnp.dot(xn, wu_ref[...], preferred_element_type=F32)
        hid = (gate * _sigmoid(gate) * up).astype(BF16)
        o_ref[out_rows, :] = (jnp.dot(hid, wd_ref[...], preferred_element_type=F32) * route_gate).astype(o_ref.dtype)

    @pl.when(t == 0)
    def _():
        def first(r, carry):
            row_copy(idx0_ref, rows_a, 0, r).start()
            return carry
        lax.fori_loop(0, tm, first, 0)

    wait_rows(rows_a, 0)
    start_rows(idx1_ref, rows_b, 1)
    ffn(rows_a, slice(0, tm))
    wait_rows(rows_b, 1)
    start_rows(idx2_ref, rows_a, 0)
    ffn(rows_b, slice(tm, 2 * tm))

    @pl.when(t == n_steps - 1)
    def _():
        wait_rows(rows_a, 0)


def expert_ffn(idx, ext, g, wg, wu, wd, tm):
    E, cap = idx.shape
    D, FF = wg.shape[1], wg.shape[2]
    steps_per_expert = cap // (2 * tm)
    n_tiles = E * cap // tm
    idx3 = idx.reshape(n_tiles, 1, tm)
    smem = lambda imap: pl.BlockSpec((1, 1, tm), imap, memory_space=pltpu.SMEM)
    wspec = lambda shape: pl.BlockSpec((None,) + shape, lambda t: (t // steps_per_expert, 0, 0))
    ye = pl.pallas_call(
        functools.partial(_ffn_kernel, steps_per_expert=steps_per_expert), grid=(n_tiles // 2,),
        in_specs=[smem(lambda t: (2 * t, 0, 0)), smem(lambda t: (2 * t + 1, 0, 0)),
                  smem(lambda t: (jnp.minimum(2 * t + 2, n_tiles - 1), 0, 0)),
                  pl.BlockSpec(memory_space=pl.ANY),
                  pl.BlockSpec((1, D), lambda t: (0, 0)),
                  wspec((D, FF)), wspec((D, FF)), wspec((FF, D))],
        out_specs=pl.BlockSpec((2 * tm, D), lambda t: (t, 0)),
        out_shape=jax.ShapeDtypeStruct((E * cap, D), BF16),
        scratch_shapes=[pltpu.VMEM((tm, D + LANES), F32), pltpu.VMEM((tm, D + LANES), F32),
                        pltpu.SemaphoreType.DMA((2,))],
        compiler_params=_cparams(("arbitrary",), 56), name="moe_ffn",
    )(idx3, idx3, idx3, ext, g.reshape(1, D), wg, wu, wd)
    return ye.reshape(E, cap, D)


COMBINE_TOKENS = 128


def _combine_kernel(s0_ref, x_ref, pos_ref, ye_hbm, o_ref, buf_ref, sem, *, win):
    t, n_tiles = pl.program_id(0), pl.num_programs(0)
    n_exp = pos_ref.shape[1]
    par = t % 2

    def window(tile, e, p):
        s0 = pl.multiple_of(s0_ref[e, tile], 16)
        return pltpu.make_async_copy(ye_hbm.at[e, pl.ds(s0, win)], buf_ref.at[p, e], sem.at[p, e])

    @pl.when(t == 0)
    def _():
        for e in range(n_exp):
            window(0, e, 0).start()

    @pl.when(t + 1 < n_tiles)
    def _():
        for e in range(n_exp):
            window(t + 1, e, 1 - par).start()

    for e in range(n_exp):
        window(t, e, par).wait()
    lane = lax.broadcasted_iota(I32, (x_ref.shape[0], win), 1)
    onehots = [jnp.where(pos_ref[:, e:e + 1] == lane + s0_ref[e, t], 1.0, 0.0).astype(BF16)
               for e in range(n_exp)]
    parts = [jnp.dot(onehots[e], buf_ref[par, e], preferred_element_type=F32) for e in range(n_exp)]
    while len(parts) > 1:
        parts = [a + b for a, b in zip(parts[0::2], parts[1::2])]
    o_ref[...] = x_ref[...] + parts[0]


def moe_combine(x, pos_t, off, ye, tt):
    N, D = x.shape
    E, cap, _ = ye.shape
    win = tt + 16
    s0 = jnp.clip((off // 16) * 16, 0, cap - win).astype(I32)
    grid_spec = pltpu.PrefetchScalarGridSpec(
        num_scalar_prefetch=1, grid=(N // tt,),
        in_specs=[pl.BlockSpec((tt, D), lambda t, *_: (t, 0)), pl.BlockSpec((tt, E), lambda t, *_: (t, 0)),
                  pl.BlockSpec(memory_space=pl.ANY)],
        out_specs=pl.BlockSpec((tt, D), lambda t, *_: (t, 0)),
        scratch_shapes=[pltpu.VMEM((2, E, win, D), BF16), pltpu.SemaphoreType.DMA((2, E))])
    return pl.pallas_call(
        functools.partial(_combine_kernel, win=win), grid_spec=grid_spec,
        out_shape=jax.ShapeDtypeStruct((N, D), F32),
        compiler_params=_cparams(("arbitrary",), 40), name="moe_combine",
    )(s0, x, pos_t, ye)


def expert_choice_moe(x, g, wr_t, wg, wu, wd):
    N, D = x.shape
    E = wr_t.shape[0]
    cap = CAPACITY * N // E
    nb = N // LANES
    aff, ext = router(x, g, wr_t, 512)
    pos3, off3, cnt3 = topk_slots(aff.reshape(E, nb, LANES), cap)
    idx = compact_slots(cnt3, cap)
    ye = expert_ffn(idx, ext, g, wg, wu, wd, min(512, cap // 2))
    return moe_combine(x, pos3.reshape(E, N).T, off3[:, ::COMBINE_TOKENS // LANES, 0], ye, COMBINE_TOKENS)


def _rope_tables(seq):
    pos = jnp.arange(seq, dtype=F32)
    inv = 1.0 / (ROPE_THETA ** (jnp.arange(0, MLA_ROPE, 2, dtype=F32) / MLA_ROPE))
    ang = pos[:, None] * inv[None, :]
    cos, sin = jnp.cos(ang), jnp.sin(ang)
    z32, z64 = jnp.zeros_like(cos), jnp.zeros((seq, LANES - MLA_ROPE), F32)
    return (jnp.concatenate([cos, cos, z64], axis=1), jnp.concatenate([-sin, z32, z64], axis=1),
            jnp.concatenate([z32, sin, z64], axis=1))


def _pad_cols(a, width):
    return jnp.pad(a, [(0, 0)] * (a.ndim - 1) + [(0, width - a.shape[-1])])


def even_layer(x, batch, seq, g_mix, w_in, w_out, lb, hg_gain, qn_g, kvn_g, w_uq, w_ukv, q_gain, k_gain):
    T, D = x.shape
    hgw = 5 * HG_WIDTH
    w_hg = w_in[:, :hgw].astype(BF16)
    w_mla = _pad_cols(w_in[:, hgw:], 6 * LANES).astype(BF16)
    proj_hg, proj_mla = norm_matmul(x, g_mix, [w_hg, w_mla], [F32, F32], 256)

    proj3 = proj_hg.reshape(batch, seq, hgw)
    rows = 256
    o_f = hgrn_direction(proj3, lb[0:1], False, rows)
    o_hg = hgrn_direction(proj3, lb[1:2], True, rows, o_fwd=o_f, gain=hg_gain)

    wq = _pad_cols(w_uq.reshape(MLA_Q_RANK, MLA_HEADS, MLA_QK), MLA_QK_PAD).reshape(MLA_Q_RANK, -1).astype(BF16)
    wkv = w_ukv.astype(BF16)
    qg = _pad_cols(q_gain.reshape(1, MLA_QK), MLA_QK_PAD)
    kg = _pad_cols(k_gain.reshape(1, MLA_QK), MLA_QK_PAD)
    q, k, v = mla_prep(proj_mla, qn_g.reshape(1, -1), kvn_g.reshape(1, -1), wq, wkv, qg, kg,
                       _rope_tables(seq), seq, 512)
    hq = MLA_HEADS * MLA_QK_PAD
    tq, tk = min(seq, 1024), min(seq, 2048)
    o_mla = attention(q.reshape(batch, seq, hq), k.reshape(batch, seq, hq),
                      v.reshape(batch, seq, hq), tq, tk)

    w_o = w_out.astype(BF16)
    return matmul_resid(x, [o_hg.reshape(T, HG_WIDTH), o_mla.reshape(T, MLA_HEADS * MLA_V)],
                        [w_o[:HG_WIDTH], w_o[HG_WIDTH:]], 512)


def odd_layer(x, batch, seq, g_mix, w_in, conv_w, conv_b, wa, ba, wx, bx, lam, w_out):
    T, D = x.shape
    (proj,) = norm_matmul(x, g_mix, [w_in.astype(BF16)], [F32], 256)
    proj3 = proj.reshape(batch, seq, 2 * D)
    sp = jax.nn.softplus(-lam.astype(F32))
    cb = conv_b.reshape(1, D)
    ts = 128
    h_f = lru_direction(proj3, conv_w, cb, wa[0].astype(BF16), ba[0:1], wx[0].astype(BF16), bx[0:1], sp[0:1],
                        False, ts)
    y = lru_direction(proj3, conv_w, cb, wa[1].astype(BF16), ba[1:2], wx[1].astype(BF16), bx[1:2], sp[1:2],
                      True, ts, h_fwd=h_f)
    return matmul_resid(x, [y.reshape(T, D)], [w_out.astype(BF16)], 512)


def _trunk(x3, norm_mix, norm_ffn, ev_w_in, ev_w_out, hg_lb_logits, hg_out_gain, mla_q_norm, mla_kv_norm,
           mla_w_uq, mla_w_ukv, mla_q_gain, mla_k_gain, od_w_in, od_conv_w, od_conv_b, rg_w_a, rg_b_a,
           rg_w_x, rg_b_x, rg_lambda, od_w_out, moe_router, moe_w_gate, moe_w_up, moe_w_down):
    batch, seq, D = x3.shape
    x = x3.reshape(batch * seq, D)
    lb_all = jnp.cumsum(jax.nn.softmax(hg_lb_logits.astype(F32), axis=1), axis=1)
    depth = norm_mix.shape[0]
    for layer in range(depth):
        j = layer // 2
        if layer % 2 == 0:
            x = even_layer(x, batch, seq, norm_mix[layer], ev_w_in[j], ev_w_out[j], lb_all[:, j],
                           hg_out_gain[j], mla_q_norm[j], mla_kv_norm[j], mla_w_uq[j], mla_w_ukv[j],
                           mla_q_gain[j], mla_k_gain[j])
        else:
            x = odd_layer(x, batch, seq, norm_mix[layer], od_w_in[j], od_conv_w[j], od_conv_b[j], rg_w_a[j],
                          rg_b_a[j], rg_w_x[j], rg_b_x[j], rg_lambda[j], od_w_out[j])
        x = expert_choice_moe(x, norm_ffn[layer], moe_router[layer].T, moe_w_gate[layer].astype(BF16),
                              moe_w_up[layer].astype(BF16), moe_w_down[layer].astype(BF16))
    return x.reshape(batch, seq, D)


def kernel(x_prompt, x_sample, norm_mix, norm_ffn, ev_w_in, ev_w_out, hg_lb_logits, hg_out_gain, mla_q_norm,
           mla_kv_norm, mla_w_uq, mla_w_ukv, mla_q_gain, mla_k_gain, od_w_in, od_conv_w, od_conv_b, rg_w_a,
           rg_b_a, rg_w_x, rg_b_x, rg_lambda, od_w_out, moe_router, moe_w_gate, moe_w_up, moe_w_down):
    params = (norm_mix, norm_ffn, ev_w_in, ev_w_out, hg_lb_logits, hg_out_gain, mla_q_norm, mla_kv_norm,
              mla_w_uq, mla_w_ukv, mla_q_gain, mla_k_gain, od_w_in, od_conv_w, od_conv_b, rg_w_a, rg_b_a,
              rg_w_x, rg_b_x, rg_lambda, od_w_out, moe_router, moe_w_gate, moe_w_up, moe_w_down)
    return (_trunk(x_prompt, *params), _trunk(x_sample, *params))
```

```python
import functools

import jax
import jax.numpy as jnp
from jax import lax
from jax.experimental import pallas as pl
from jax.experimental.pallas import tpu as pltpu

F32 = jnp.float32
BF16 = jnp.bfloat16
I32 = jnp.int32
EPS = 1e-6
HIGHEST = lax.Precision.HIGHEST
NT_DIMS = (((1,), (1,)), ((), ()))
TN_DIMS = (((0,), (0,)), ((), ()))

D_MODEL = 1024
HG_HEADS = 4
HG_DK = 128
HG_WIDTH = 512
HG_CHUNK = 64
HG_SUB = 16
HG_EXP_CLAMP = 80.0
MLA_HEADS = 4
MLA_Q_RANK = 384
MLA_KV_RANK = 256
MLA_NOPE = 128
MLA_ROPE = 64
MLA_V = 128
MLA_QK = MLA_NOPE + MLA_ROPE
MLA_QK_PAD = 256
ROPE_THETA = 10000.0
LRU_BLOCKS = 4
LRU_BW = 256
LRU_C = 8.0
N_EXPERTS = 16
CAPACITY = 2
LANES = 128
LOG2_E = 1.4426950408889634
MIB = 1024 * 1024


def _cparams(semantics, vmem_mib):
    return pltpu.CompilerParams(dimension_semantics=semantics, vmem_limit_bytes=vmem_mib * MIB)


def _sigmoid(x):
    return 1.0 / (1.0 + jnp.exp(-x))


def _rms_scale(x, width):
    return lax.rsqrt(jnp.sum(x * x, axis=-1, keepdims=True) * (1.0 / width) + EPS)


def _norm_matmul_kernel(x_ref, g_ref, *refs, n_out):
    x = x_ref[...]
    h = (x * _rms_scale(x, x.shape[-1]) * g_ref[...]).astype(BF16)
    for w_ref, o_ref in zip(refs[:n_out], refs[n_out:]):
        o_ref[...] = jnp.dot(h, w_ref[...], preferred_element_type=F32).astype(o_ref.dtype)


def norm_matmul(x, g, ws, out_dtypes, tm):
    T, D = x.shape
    n = len(ws)
    in_specs = [pl.BlockSpec((tm, D), lambda i: (i, 0)), pl.BlockSpec((1, D), lambda i: (0, 0))]
    in_specs += [pl.BlockSpec(w.shape, lambda i: (0, 0)) for w in ws]
    out_specs = [pl.BlockSpec((tm, w.shape[1]), lambda i: (i, 0)) for w in ws]
    out_shape = [jax.ShapeDtypeStruct((T, w.shape[1]), dt) for w, dt in zip(ws, out_dtypes)]
    return pl.pallas_call(
        functools.partial(_norm_matmul_kernel, n_out=n),
        grid=(T // tm,), in_specs=in_specs, out_specs=out_specs, out_shape=out_shape,
        compiler_params=_cparams(("parallel",), 56), name="norm_matmul",
    )(x, g.reshape(1, D), *ws)


def _matmul_resid_kernel(r_ref, *refs, n_in):
    acc = r_ref[...]
    for a_ref, w_ref in zip(refs[:n_in], refs[n_in:2 * n_in]):
        acc = acc + jnp.dot(a_ref[...], w_ref[...], preferred_element_type=F32)
    refs[2 * n_in][...] = acc


def matmul_resid(resid, acts, ws, tm):
    T, D = resid.shape
    n = len(acts)
    in_specs = [pl.BlockSpec((tm, D), lambda i: (i, 0))]
    in_specs += [pl.BlockSpec((tm, a.shape[1]), lambda i: (i, 0)) for a in acts]
    in_specs += [pl.BlockSpec(w.shape, lambda i: (0, 0)) for w in ws]
    return pl.pallas_call(
        functools.partial(_matmul_resid_kernel, n_in=n),
        grid=(T // tm,), in_specs=in_specs, out_specs=pl.BlockSpec((tm, D), lambda i: (i, 0)),
        out_shape=jax.ShapeDtypeStruct((T, D), F32),
        compiler_params=_cparams(("parallel",), 40), name="matmul_resid",
    )(resid, *acts, *ws)


def _hgrn_kernel(*refs, reverse, final, n_chunk):
    if final:
        q_ref, v_ref, fz_ref, lb_ref, g_ref, of_ref, gain_ref, o_ref, st_ref = refs
    else:
        q_ref, v_ref, fz_ref, lb_ref, o_ref, st_ref = refs
    C, SB = HG_CHUNK, HG_SUB

    @pl.when(pl.program_id(1) == 0)
    def _():
        st_ref[...] = jnp.zeros_like(st_ref)

    row = lax.broadcasted_iota(I32, (C, C), 0)
    col = lax.broadcasted_iota(I32, (C, C), 1)
    blk_of_row = row // SB
    if reverse:
        tri = col >= row
        ref_of_row = blk_of_row * SB
        causal = col >= row
    else:
        tri = col <= row
        ref_of_row = blk_of_row * SB + (SB - 1)
        causal = col <= row
    ref_sum = (col >= ref_of_row) if reverse else (col <= ref_of_row)
    cum_mat = jnp.concatenate([jnp.where(tri, 1.0, 0.0), jnp.where(ref_sum, 1.0, 0.0)], axis=0)
    row_blk = lax.broadcasted_iota(I32, (C, HG_WIDTH), 0) // SB

    heads = [slice(h * HG_DK, (h + 1) * HG_DK) for h in range(HG_HEADS)]
    order = [((n_chunk - 1 - ci) if reverse else ci) for ci in range(n_chunk)]
    lb = lb_ref[...]

    per_chunk = []
    for c in order:
        rows = slice(c * C, (c + 1) * C)
        q = q_ref[rows, :] * (HG_DK ** -0.5)
        f = lb + (1.0 - lb) * _sigmoid(fz_ref[rows, :])
        logf = jnp.log(f)
        k = 1.0 - f
        bb = jnp.dot(cum_mat, logf, precision=HIGHEST, preferred_element_type=F32)
        b, b_blk = bb[:C], bb[C:]
        kt = k * jnp.exp(b_blk - b)
        q_parts, k_parts = [], []
        for j in range(C // SB):
            jr = j * SB if reverse else j * SB + SB - 1
            q_parts.append((q * jnp.exp(jnp.minimum(b - b[jr:jr + 1, :], HG_EXP_CLAMP))).astype(BF16))
            k_parts.append(jnp.where(row_blk == j, kt, 0.0).astype(BF16))
        b_end = b[0:1, :] if reverse else b[C - 1:C, :]
        vb = v_ref[rows, :].astype(BF16)
        intra = []
        for cs in heads:
            scores = lax.dot_general(jnp.concatenate([p[:, cs] for p in q_parts], axis=1),
                                     jnp.concatenate([p[:, cs] for p in k_parts], axis=1),
                                     NT_DIMS, preferred_element_type=F32)
            scores = jnp.where(causal, scores, 0.0).astype(BF16)
            intra.append(jnp.dot(scores, vb[:, cs], preferred_element_type=F32))
        per_chunk.append((rows, intra, (q * jnp.exp(b)).astype(BF16), (k * jnp.exp(b_end - b)).astype(BF16),
                          jnp.exp(b_end), vb))

    states = [st_ref[h] for h in range(HG_HEADS)]
    for rows, intra, q_dec, k_hat, decay_end, vb in per_chunk:
        for h, cs in enumerate(heads):
            o = intra[h] + lax.dot_general(q_dec[:, cs], states[h].astype(BF16), NT_DIMS,
                                           preferred_element_type=F32)
            states[h] = states[h] * decay_end[:, cs] + lax.dot_general(
                vb[:, cs], k_hat[:, cs], TN_DIMS, preferred_element_type=F32)
            if final:
                o = o + of_ref[rows, cs]
                y = o * _rms_scale(o, HG_DK) * gain_ref[...]
                g = g_ref[rows, cs]
                o_ref[rows, cs] = (y * (g * _sigmoid(g))).astype(o_ref.dtype)
            else:
                o_ref[rows, cs] = o
    for h in range(HG_HEADS):
        st_ref[h] = states[h]


def hgrn_direction(proj, lb, reverse, rows, o_fwd=None, gain=None):
    B, S, _ = proj.shape
    W = HG_WIDTH
    nblk = S // rows
    final = o_fwd is not None

    def rowmap(colblk):
        if reverse:
            return lambda b, c: (b, nblk - 1 - c, colblk)
        return lambda b, c: (b, c, colblk)

    blk = (None, rows, W)
    in_specs = [pl.BlockSpec(blk, rowmap(0)), pl.BlockSpec(blk, rowmap(1)),
                pl.BlockSpec(blk, rowmap(4 if reverse else 3)), pl.BlockSpec((1, W), lambda b, c: (0, 0))]
    args = [proj, proj, proj, lb]
    if final:
        in_specs += [pl.BlockSpec(blk, rowmap(2)), pl.BlockSpec(blk, rowmap(0)),
                     pl.BlockSpec((1, HG_DK), lambda b, c: (0, 0))]
        args += [proj, o_fwd, gain.reshape(1, HG_DK)]
    return pl.pallas_call(
        functools.partial(_hgrn_kernel, reverse=reverse, final=final, n_chunk=rows // HG_CHUNK),
        grid=(B, nblk), in_specs=in_specs, out_specs=pl.BlockSpec(blk, rowmap(0)),
        out_shape=jax.ShapeDtypeStruct((B, S, W), BF16 if final else F32),
        scratch_shapes=[pltpu.VMEM((HG_HEADS, HG_DK, HG_DK), F32)],
        compiler_params=_cparams(("parallel", "arbitrary"), 40),
        name="hgrn_bwd" if reverse else "hgrn_fwd",
    )(*args)


def _rope(x, cos_t, sin_a, sin_b):
    return x * cos_t + pltpu.roll(x, LANES - MLA_ROPE // 2, 1) * sin_a + pltpu.roll(x, MLA_ROPE // 2, 1) * sin_b


def _mla_prep_kernel(c_ref, qn_ref, kvn_ref, wq_ref, wkv_ref, qg_ref, kg_ref, cos_ref, sa_ref, sb_ref,
                     q_ref, k_ref, v_ref):
    c = c_ref[...]
    cq = c[:, :MLA_Q_RANK]
    ckv = c[:, MLA_Q_RANK:MLA_Q_RANK + MLA_KV_RANK]
    kpe = c[:, MLA_Q_RANK + MLA_KV_RANK:]
    qn = (cq * _rms_scale(cq, MLA_Q_RANK) * qn_ref[...]).astype(BF16)
    kvn = (ckv * _rms_scale(ckv, MLA_KV_RANK) * kvn_ref[...]).astype(BF16)
    qf = jnp.dot(qn, wq_ref[...], preferred_element_type=F32)
    kvf = jnp.dot(kvn, wkv_ref[...], preferred_element_type=F32)
    cos_t, sin_a, sin_b = cos_ref[...], sa_ref[...], sb_ref[...]
    qg, kg = qg_ref[...], kg_ref[...]
    sm_scale = (MLA_QK ** -0.5) * LOG2_E
    kpe_ss = jnp.sum(kpe * kpe, axis=-1, keepdims=True)
    for h in range(MLA_HEADS):
        lo = h * MLA_QK_PAD
        q_nope, q_rope = qf[:, lo:lo + LANES], qf[:, lo + LANES:lo + 2 * LANES]
        qs = lax.rsqrt((jnp.sum(q_nope * q_nope, axis=-1, keepdims=True)
                        + jnp.sum(q_rope * q_rope, axis=-1, keepdims=True)) * (1.0 / MLA_QK) + EPS)
        q_ref[:, lo:lo + LANES] = (q_nope * qs * qg[:, :LANES] * sm_scale).astype(BF16)
        q_ref[:, lo + LANES:lo + 2 * LANES] = (
            _rope(q_rope * qs * qg[:, LANES:], cos_t, sin_a, sin_b) * sm_scale).astype(BF16)
        k_nope = kvf[:, lo:lo + LANES]
        ks = lax.rsqrt((jnp.sum(k_nope * k_nope, axis=-1, keepdims=True) + kpe_ss) * (1.0 / MLA_QK) + EPS)
        k_ref[:, lo:lo + LANES] = (k_nope * ks * kg[:, :LANES]).astype(BF16)
        k_ref[:, lo + LANES:lo + 2 * LANES] = _rope(kpe * ks * kg[:, LANES:], cos_t, sin_a, sin_b).astype(BF16)
        v_ref[:, lo:lo + LANES] = kvf[:, lo + LANES:lo + 2 * LANES].astype(BF16)
        v_ref[:, lo + LANES:lo + 2 * LANES] = jnp.ones((c.shape[0], LANES), BF16)


def mla_prep(c, qn_g, kvn_g, wq, wkv, qg, kg, tables, seq, tm):
    T, CW = c.shape
    nseq = seq // tm
    full = lambda a: pl.BlockSpec(a.shape, lambda i: (0, 0))
    tab = pl.BlockSpec((tm, LANES), lambda i: (i % nseq, 0))
    HW = MLA_HEADS * MLA_QK_PAD
    return pl.pallas_call(
        _mla_prep_kernel, grid=(T // tm,),
        in_specs=[pl.BlockSpec((tm, CW), lambda i: (i, 0)), full(qn_g), full(kvn_g), full(wq), full(wkv),
                  full(qg), full(kg), tab, tab, tab],
        out_specs=[pl.BlockSpec((tm, HW), lambda i: (i, 0))] * 3,
        out_shape=[jax.ShapeDtypeStruct((T, HW), BF16)] * 3,
        compiler_params=_cparams(("parallel",), 40), name="mla_prep",
    )(c, qn_g, kvn_g, wq, wkv, qg, kg, *tables)


ATTN_PARTS = 4


def _attn_kernel(q_ref, k_ref, v_ref, o_ref, m_ref, l_ref, acc_ref):
    ki = pl.program_id(3)

    @pl.when(ki == 0)
    def _():
        m_ref[...] = jnp.full_like(m_ref, -jnp.inf)
        l_ref[...] = jnp.zeros_like(l_ref)
        acc_ref[...] = jnp.zeros_like(acc_ref)

    k, v = k_ref[...], v_ref[...]
    part = q_ref.shape[0] // ATTN_PARTS
    rows = [slice(i * part, (i + 1) * part) for i in range(ATTN_PARTS)]
    scores = lambda r: lax.dot_general(q_ref[r, :], k, NT_DIMS, preferred_element_type=F32)
    s_next = scores(rows[0])
    for i, r in enumerate(rows):
        s = s_next
        if i + 1 < ATTN_PARTS:
            s_next = scores(rows[i + 1])
        m_prev = m_ref[r, :]
        m_new = jnp.maximum(m_prev, jnp.max(s, axis=-1, keepdims=True))
        alpha = jnp.exp2(m_prev - m_new)
        p = jnp.exp2(s - m_new)
        l_ref[r, :] = alpha * l_ref[r, :] + jnp.sum(p, axis=-1, keepdims=True)
        acc_ref[r, :] = alpha * acc_ref[r, :] + jnp.dot(p.astype(BF16), v, preferred_element_type=F32)
        m_ref[r, :] = m_new

    @pl.when(ki == pl.num_programs(3) - 1)
    def _():
        o_ref[...] = (acc_ref[...] / l_ref[...]).astype(o_ref.dtype)


def attention(q, k, v, tq, tk):
    B, S, _ = q.shape
    return pl.pallas_call(
        _attn_kernel, grid=(B, MLA_HEADS, S // tq, S // tk),
        in_specs=[pl.BlockSpec((None, tq, MLA_QK_PAD), lambda b, h, i, j: (b, i, h)),
                  pl.BlockSpec((None, tk, MLA_QK_PAD), lambda b, h, i, j: (b, j, h)),
                  pl.BlockSpec((None, tk, MLA_V), lambda b, h, i, j: (b, j, 2 * h))],
        out_specs=pl.BlockSpec((None, tq, MLA_V), lambda b, h, i, j: (b, i, h)),
        out_shape=jax.ShapeDtypeStruct((B, S, MLA_HEADS * MLA_V), BF16),
        scratch_shapes=[pltpu.VMEM((tq, 1), F32), pltpu.VMEM((tq, 1), F32), pltpu.VMEM((tq, MLA_V), F32)],
        compiler_params=_cparams(("parallel", "parallel", "parallel", "arbitrary"), 48), name="mla_attention",
    )(q, k, v)


SUBLANES = 8


def _scan_rows(a, u, h_in, reverse):
    n_group = a.shape[0] // SUBLANES
    sub = lax.broadcasted_iota(I32, (SUBLANES, a.shape[1]), 0)
    scanned = []
    for g in range(n_group):
        rows = slice(g * SUBLANES, (g + 1) * SUBLANES)
        ag, ug = a[rows], u[rows]
        k = 1
        while k < SUBLANES:
            shift, valid = (SUBLANES - k, sub < SUBLANES - k) if reverse else (k, sub >= k)
            a_s, u_s = pltpu.roll(ag, shift, 0), pltpu.roll(ug, shift, 0)
            ug = ug + ag * jnp.where(valid, u_s, 0.0)
            ag = ag * jnp.where(valid, a_s, 1.0)
            k *= 2
        scanned.append((ag, ug))
    outs = [None] * n_group
    h = h_in
    for g in (range(n_group - 1, -1, -1) if reverse else range(n_group)):
        ag, ug = scanned[g]
        outs[g] = ug + ag * h
        h = outs[g][0:1] if reverse else outs[g][SUBLANES - 1:SUBLANES]
    return jnp.concatenate(outs, axis=0), h


def _lru_kernel(*refs, reverse, final):
    if final:
        (x_ref, prev_ref, next_ref, cw_ref, cb_ref, wa_ref, ba_ref, wx_ref, bx_ref, sp_ref,
         gate_ref, hf_ref, o_ref, carry_ref) = refs
    else:
        (x_ref, prev_ref, next_ref, cw_ref, cb_ref, wa_ref, ba_ref, wx_ref, bx_ref, sp_ref,
         o_ref, carry_ref) = refs
    step, nstep = pl.program_id(1), pl.num_programs(1)
    blk = (nstep - 1 - step) if reverse else step

    @pl.when(step == 0)
    def _():
        carry_ref[...] = jnp.zeros_like(carry_ref)

    x = x_ref[...]
    ts = x.shape[0]
    rowi = lax.broadcasted_iota(I32, x.shape, 0)
    has_prev = jnp.where(blk > 0, 1.0, 0.0)
    has_next = jnp.where(blk < nstep - 1, 1.0, 0.0)
    p6, p7, n0 = prev_ref[6:7, :] * has_prev, prev_ref[7:8, :] * has_prev, next_ref[0:1, :] * has_next
    xm1 = jnp.where(rowi == 0, p7, pltpu.roll(x, 1, 0))
    xm2 = jnp.where(rowi == 0, p6, jnp.where(rowi == 1, p7, pltpu.roll(x, 2, 0)))
    xp1 = jnp.where(rowi == ts - 1, n0, pltpu.roll(x, ts - 1, 0))
    cw = cw_ref[...]
    xc = xm2 * cw[0:1, :] + xm1 * cw[1:2, :] + x * cw[2:3, :] + xp1 * cw[3:4, :] + cb_ref[...]
    xcb = xc.astype(BF16)

    def block_diag(w_ref, b_ref):
        outs = [jnp.dot(xcb[:, n * LRU_BW:(n + 1) * LRU_BW], w_ref[n], preferred_element_type=F32)
                for n in range(LRU_BLOCKS)]
        return jnp.concatenate(outs, axis=1) + b_ref[...]

    r = _sigmoid(block_diag(wa_ref, ba_ref))
    i = _sigmoid(block_diag(wx_ref, bx_ref))
    log_a = (-LRU_C) * r * sp_ref[...]
    a = jnp.exp(log_a)
    u = jnp.sqrt(1.0 - a * a) * (i * xc)
    h, carry_ref[...] = _scan_rows(a, u, carry_ref[...], reverse)
    if final:
        g = gate_ref[...]
        gelu = 0.5 * g * (1.0 + jnp.tanh(0.7978845608028654 * (g + 0.044715 * (g * g * g))))
        o_ref[...] = ((h + hf_ref[...]) * gelu).astype(o_ref.dtype)
    else:
        o_ref[...] = h


def lru_direction(proj, cw, cb, wa, ba, wx, bx, sp, reverse, ts, h_fwd=None):
    B, S, W2 = proj.shape
    W = W2 // 2
    nblk = S // ts
    final = h_fwd is not None
    sub = ts // 8

    def pos(c):
        return (nblk - 1 - c) if reverse else c

    tile = (None, ts, W)
    halo = (None, 8, W)
    vec = lambda r: pl.BlockSpec((r, W), lambda b, c: (0, 0))
    wspec = pl.BlockSpec((LRU_BLOCKS, LRU_BW, LRU_BW), lambda b, c: (0, 0, 0))
    in_specs = [pl.BlockSpec(tile, lambda b, c: (b, pos(c), 1)),
                pl.BlockSpec(halo, lambda b, c: (b, jnp.maximum(pos(c) * sub - 1, 0), 1)),
                pl.BlockSpec(halo, lambda b, c: (b, jnp.minimum((pos(c) + 1) * sub, S // 8 - 1), 1)),
                vec(4), vec(1), wspec, vec(1), wspec, vec(1), vec(1)]
    args = [proj, proj, proj, cw, cb, wa, ba, wx, bx, sp]
    if final:
        in_specs += [pl.BlockSpec(tile, lambda b, c: (b, pos(c), 0)), pl.BlockSpec(tile, lambda b, c: (b, pos(c), 0))]
        args += [proj, h_fwd]
    return pl.pallas_call(
        functools.partial(_lru_kernel, reverse=reverse, final=final),
        grid=(B, nblk), in_specs=in_specs, out_specs=pl.BlockSpec(tile, lambda b, c: (b, pos(c), 0)),
        out_shape=jax.ShapeDtypeStruct((B, S, W), BF16 if final else F32),
        scratch_shapes=[pltpu.VMEM((1, W), F32)],
        compiler_params=_cparams(("parallel", "arbitrary"), 48),
        name="lru_bwd" if reverse else "lru_fwd",
    )(*args)


def _router_kernel(x_ref, g_ref, wr_ref, aff_ref, ext_ref):
    x = x_ref[...]
    d = x.shape[1]
    xn = x * _rms_scale(x, d) * g_ref[...]
    logits = lax.dot_general(wr_ref[...], xn, NT_DIMS, precision=HIGHEST, preferred_element_type=F32)
    e = jnp.exp(logits - jnp.max(logits, axis=0, keepdims=True))
    aff = e / jnp.sum(e, axis=0, keepdims=True)
    aff_ref[...] = aff
    half = d // 2
    as_bits = lambda v: pltpu.bitcast(v.astype(BF16).astype(F32), I32)
    ext_ref[:, :half] = lax.shift_right_logical(as_bits(xn[:, :half]), jnp.int32(16)) | as_bits(xn[:, half:])
    aff_rows = jnp.concatenate([aff, jnp.zeros((LANES - aff.shape[0], aff.shape[1]), F32)], axis=0).T
    ext_ref[:, half:] = pltpu.bitcast(aff_rows, I32)


def router(x, g, wr_t, tm):
    N, D = x.shape
    E = wr_t.shape[0]
    return pl.pallas_call(
        _router_kernel, grid=(N // tm,),
        in_specs=[pl.BlockSpec((tm, D), lambda i: (i, 0)), pl.BlockSpec((1, D), lambda i: (0, 0)),
                  pl.BlockSpec((E, D), lambda i: (0, 0))],
        out_specs=[pl.BlockSpec((E, tm), lambda i: (0, i)), pl.BlockSpec((tm, D // 2 + LANES), lambda i: (i, 0))],
        out_shape=[jax.ShapeDtypeStruct((E, N), F32), jax.ShapeDtypeStruct((N, D // 2 + LANES), I32)],
        compiler_params=_cparams(("parallel",), 32), name="moe_router",
    )(x, g.reshape(1, D), wr_t)


def _topk_kernel(aff_ref, pos_ref, off_ref, cnt_ref, *, cap):
    bits = pltpu.bitcast(aff_ref[...], I32)
    nb = bits.shape[0]

    def count(pred):
        return jnp.sum(jnp.sum(jnp.where(pred, 1.0, 0.0), axis=1, keepdims=True), axis=0, keepdims=True)

    def bisect(i, lo):
        cand = lo | lax.shift_left(jnp.int32(1), jnp.int32(30) - i)
        return jnp.where(count(bits >= cand) >= cap, cand, lo)

    thr = lax.fori_loop(0, 31, bisect, jnp.zeros((1, 1), I32))

    r128 = lax.broadcasted_iota(I32, (LANES, LANES), 0)
    c128 = lax.broadcasted_iota(I32, (LANES, LANES), 1)
    upper = jnp.where(r128 <= c128, 1.0, 0.0).astype(BF16)
    rn = lax.broadcasted_iota(I32, (nb, nb), 0)
    cn = lax.broadcasted_iota(I32, (nb, nb), 1)
    strict_lower = jnp.where(cn < rn, 1.0, 0.0).astype(BF16)

    def running_count(m):
        incl = jnp.dot(m.astype(BF16), upper, preferred_element_type=F32)
        tot = jnp.broadcast_to(incl[:, LANES - 1:LANES], (nb, LANES)).astype(BF16)
        return incl, jnp.dot(strict_lower, tot, preferred_element_type=F32)

    gt = jnp.where(bits > thr, 1.0, 0.0)
    eq = jnp.where(bits == thr, 1.0, 0.0)
    need = cap - count(bits > thr)
    eq_incl, eq_off = running_count(eq)
    sel = gt + eq * jnp.where(eq_incl - eq + eq_off < need, 1.0, 0.0)
    incl, off = running_count(sel)
    cnt = incl + off
    pos_ref[...] = jnp.where(sel > 0.0, cnt - 1.0, -1.0).astype(I32)
    off_ref[...] = off.astype(I32)
    cnt_ref[...] = cnt


def topk_slots(aff3, cap):
    E, NB, _ = aff3.shape
    spec = pl.BlockSpec((None, NB, LANES), lambda e: (e, 0, 0))
    return pl.pallas_call(
        functools.partial(_topk_kernel, cap=cap), grid=(E,), in_specs=[spec], out_specs=[spec, spec, spec],
        out_shape=[jax.ShapeDtypeStruct((E, NB, LANES), I32), jax.ShapeDtypeStruct((E, NB, LANES), I32),
                   jax.ShapeDtypeStruct((E, NB, LANES), F32)],
        compiler_params=_cparams(("parallel",), 32), name="moe_topk",
    )(aff3)


def _compact_kernel(blo_ref, bhi_ref, cnt_ref, idx_ref, *, n_chunk):
    e = pl.program_id(0)
    slot = lax.broadcasted_iota(I32, (LANES, LANES), 0)

    def chunk(c, carry):
        lo, hi = blo_ref[e, c], bhi_ref[e, c]
        sigma = (slot + c * LANES).astype(F32)

        def blk(b, acc):
            return acc + jnp.where(cnt_ref[b] <= sigma, 1.0, 0.0)

        acc = lax.fori_loop(lo, hi + 1, blk, jnp.zeros((LANES, LANES), F32))
        col = jnp.sum(acc, axis=1, keepdims=True) + (lo * LANES).astype(F32)
        idx_ref[c] = jnp.broadcast_to(col, (LANES, LANES)).T[0:1, :].astype(I32)
        return carry

    lax.fori_loop(0, n_chunk, chunk, 0)


def compact_slots(cnt3, cap):
    E, NB, _ = cnt3.shape
    n_chunk = cap // LANES
    starts = jnp.arange(n_chunk, dtype=F32) * LANES
    blo = jnp.sum(cnt3[:, :, LANES - 1, None] <= starts[None, None, :], axis=1).astype(I32)
    bhi = jnp.sum(cnt3[:, :, 0, None] <= starts[None, None, :] + (LANES - 1), axis=1).astype(I32) - 1
    grid_spec = pltpu.PrefetchScalarGridSpec(
        num_scalar_prefetch=2, grid=(E,),
        in_specs=[pl.BlockSpec((None, NB, 1, LANES), lambda e, *_: (e, 0, 0, 0))],
        out_specs=pl.BlockSpec((None, n_chunk, 1, LANES), lambda e, *_: (e, 0, 0, 0)))
    idx = pl.pallas_call(
        functools.partial(_compact_kernel, n_chunk=n_chunk), grid_spec=grid_spec,
        out_shape=jax.ShapeDtypeStruct((E, n_chunk, 1, LANES), I32),
        compiler_params=_cparams(("arbitrary",), 32), name="moe_compact",
    )(blo, bhi, cnt3.reshape(E, NB, 1, LANES))
    return idx.reshape(E, cap)


def _ffn_kernel(idx0_ref, idx1_ref, idx2_ref, ext_hbm, wg_ref, wu_ref, wd_ref, o_ref, rows_a, rows_b, sem,
                *, steps_per_expert):
    t, n_steps = pl.program_id(0), pl.num_programs(0)
    tm, d = rows_a.shape[0], o_ref.shape[1]

    def row_copy(ids_ref, rows_ref, s, r):
        return pltpu.make_async_copy(ext_hbm.at[pl.ds(ids_ref[0, 0, r], 1)], rows_ref.at[pl.ds(r, 1)], sem.at[s])

    def start_rows(ids_ref, rows_ref, s):
        for r in range(tm):
            row_copy(ids_ref, rows_ref, s, r).start()

    def wait_rows(rows_ref, s):
        pltpu.make_async_copy(ext_hbm.at[pl.ds(0, tm)], rows_ref, sem.at[s]).wait()

    def ffn(rows_ref, out_rows):
        ext = rows_ref[...]
        half = d // 2
        words = ext[:, :half]
        xn = jnp.concatenate([pltpu.bitcast(lax.shift_left(words, jnp.int32(16)), F32),
                              pltpu.bitcast(words & jnp.int32(-65536), F32)], axis=1).astype(BF16)
        lane = lax.broadcasted_iota(I32, (tm, LANES), 1)
        route_gate = jnp.sum(jnp.where(lane == t // steps_per_expert, pltpu.bitcast(ext[:, half:], F32), 0.0),
                             axis=1, keepdims=True)
        gate = jnp.dot(xn, wg_ref[...], preferred_element_type=F32)
        up = jnp.dot(xn, wu_ref[...], preferred_element_type=F32)
        hid = (gate * _sigmoid(gate) * up).astype(BF16)
        o_ref[out_rows, :] = (jnp.dot(hid, wd_ref[...], preferred_element_type=F32) * route_gate).astype(o_ref.dtype)

    @pl.when(t == 0)
    def _():
        def first(r, carry):
            row_copy(idx0_ref, rows_a, 0, r).start()
            return carry
        lax.fori_loop(0, tm, first, 0)

    wait_rows(rows_a, 0)
    start_rows(idx1_ref, rows_b, 1)
    ffn(rows_a, slice(0, tm))
    wait_rows(rows_b, 1)
    start_rows(idx2_ref, rows_a, 0)
    ffn(rows_b, slice(tm, 2 * tm))

    @pl.when(t == n_steps - 1)
    def _():
        wait_rows(rows_a, 0)


def expert_ffn(idx, ext, wg, wu, wd, tm):
    E, cap = idx.shape
    D, FF = wg.shape[1], wg.shape[2]
    steps_per_expert = cap // (2 * tm)
    n_tiles = E * cap // tm
    idx3 = idx.reshape(n_tiles, 1, tm)
    smem = lambda imap: pl.BlockSpec((1, 1, tm), imap, memory_space=pltpu.SMEM)
    wspec = lambda shape: pl.BlockSpec((None,) + shape, lambda t: (t // steps_per_expert, 0, 0))
    ye = pl.pallas_call(
        functools.partial(_ffn_kernel, steps_per_expert=steps_per_expert), grid=(n_tiles // 2,),
        in_specs=[smem(lambda t: (2 * t, 0, 0)), smem(lambda t: (2 * t + 1, 0, 0)),
                  smem(lambda t: (jnp.minimum(2 * t + 2, n_tiles - 1), 0, 0)),
                  pl.BlockSpec(memory_space=pl.ANY),
                  wspec((D, FF)), wspec((D, FF)), wspec((FF, D))],
        out_specs=pl.BlockSpec((2 * tm, D), lambda t: (t, 0)),
        out_shape=jax.ShapeDtypeStruct((E * cap, D), BF16),
        scratch_shapes=[pltpu.VMEM((tm, ext.shape[1]), I32), pltpu.VMEM((tm, ext.shape[1]), I32),
                        pltpu.SemaphoreType.DMA((2,))],
        compiler_params=_cparams(("arbitrary",), 56), name="moe_ffn",
    )(idx3, idx3, idx3, ext, wg, wu, wd)
    return ye.reshape(E, cap, D)


COMBINE_TOKENS = 128


COMBINE_ALIGN = 16
COMBINE_SMALL = 48


def _combine_kernel(s0_small_ref, s0_full_ref, needs_full_ref, x_ref, pos_ref, ye_hbm, o_ref,
                    small_ref, full_ref, sem_small, sem_full):
    t, n_tiles = pl.program_id(0), pl.num_programs(0)
    n_exp = pos_ref.shape[1]
    small, full = small_ref.shape[1] // n_exp, full_ref.shape[1]
    par = t % 2

    def small_window(tile, e, p):
        s0 = pl.multiple_of(s0_small_ref[e, tile], COMBINE_ALIGN)
        return pltpu.make_async_copy(ye_hbm.at[e, pl.ds(s0, small)], small_ref.at[p, pl.ds(e * small, small)],
                                     sem_small.at[p, e])

    def full_window(e):
        s0 = pl.multiple_of(s0_full_ref[e, t], COMBINE_ALIGN)
        return pltpu.make_async_copy(ye_hbm.at[e, pl.ds(s0, full)], full_ref.at[e], sem_full.at[e])

    def scatter_sum(s0_ref, win, rows_of):
        lane = lax.broadcasted_iota(I32, (x_ref.shape[0], win), 1)
        onehots = [jnp.where(pos_ref[:, e:e + 1] == lane + s0_ref[e, t], 1.0, 0.0).astype(BF16)
                   for e in range(n_exp)]
        parts = [jnp.dot(onehots[e], rows_of(e), preferred_element_type=F32) for e in range(n_exp)]
        while len(parts) > 1:
            parts = [a + b for a, b in zip(parts[0::2], parts[1::2])]
        return x_ref[...] + parts[0]

    @pl.when(t == 0)
    def _():
        for e in range(n_exp):
            small_window(0, e, 0).start()

    @pl.when(t + 1 < n_tiles)
    def _():
        for e in range(n_exp):
            small_window(t + 1, e, 1 - par).start()

    for e in range(n_exp):
        small_window(t, e, par).wait()

    @pl.when(needs_full_ref[t] == 0)
    def _():
        width = n_exp * small
        col = lax.broadcasted_iota(I32, (1, width), 1)
        owner = col // small
        target = col - owner * small
        wanted = jnp.full((x_ref.shape[0], width), -2, I32)
        for e in range(n_exp):
            target = target + jnp.where(owner == e, s0_small_ref[e, t], 0)
            wanted = jnp.where(owner == e, pos_ref[:, e:e + 1], wanted)
        onehot = jnp.where(wanted == target, 1.0, 0.0).astype(BF16)
        o_ref[...] = x_ref[...] + jnp.dot(onehot, small_ref[par], preferred_element_type=F32)

    @pl.when(needs_full_ref[t] != 0)
    def _():
        for e in range(n_exp):
            full_window(e).start()
        for e in range(n_exp):
            full_window(e).wait()
        o_ref[...] = scatter_sum(s0_full_ref, full, lambda e: full_ref[e])


def moe_combine(x, pos_t, off, ye, tt):
    N, D = x.shape
    E, cap, _ = ye.shape
    full = tt + COMBINE_ALIGN
    small = min(COMBINE_SMALL, full)
    aligned = (off // COMBINE_ALIGN) * COMBINE_ALIGN
    s0_small = jnp.clip(aligned, 0, cap - small).astype(I32)
    s0_full = jnp.clip(aligned, 0, cap - full).astype(I32)
    end = jnp.concatenate([off[:, 1:], jnp.full((E, 1), cap, off.dtype)], axis=1)
    needs_full = jnp.any(end > s0_small + small, axis=0).astype(I32)
    grid_spec = pltpu.PrefetchScalarGridSpec(
        num_scalar_prefetch=3, grid=(N // tt,),
        in_specs=[pl.BlockSpec((tt, D), lambda t, *_: (t, 0)), pl.BlockSpec((tt, E), lambda t, *_: (t, 0)),
                  pl.BlockSpec(memory_space=pl.ANY)],
        out_specs=pl.BlockSpec((tt, D), lambda t, *_: (t, 0)),
        scratch_shapes=[pltpu.VMEM((2, E * small, D), BF16), pltpu.VMEM((E, full, D), BF16),
                        pltpu.SemaphoreType.DMA((2, E)), pltpu.SemaphoreType.DMA((E,))])
    return pl.pallas_call(
        _combine_kernel, grid_spec=grid_spec,
        out_shape=jax.ShapeDtypeStruct((N, D), F32),
        compiler_params=_cparams(("arbitrary",), 40), name="moe_combine",
    )(s0_small, s0_full, needs_full, x, pos_t, ye)


def expert_choice_moe(x, g, wr_t, wg, wu, wd):
    N, D = x.shape
    E = wr_t.shape[0]
    cap = CAPACITY * N // E
    nb = N // LANES
    aff, ext = router(x, g, wr_t, 512)
    pos3, off3, cnt3 = topk_slots(aff.reshape(E, nb, LANES), cap)
    idx = compact_slots(cnt3, cap)
    ye = expert_ffn(idx, ext, wg, wu, wd, min(512, cap // 2))
    return moe_combine(x, pos3.reshape(E, N).T, off3[:, ::COMBINE_TOKENS // LANES, 0], ye, COMBINE_TOKENS)


def _rope_tables(seq):
    pos = jnp.arange(seq, dtype=F32)
    inv = 1.0 / (ROPE_THETA ** (jnp.arange(0, MLA_ROPE, 2, dtype=F32) / MLA_ROPE))
    ang = pos[:, None] * inv[None, :]
    cos, sin = jnp.cos(ang), jnp.sin(ang)
    z32, z64 = jnp.zeros_like(cos), jnp.zeros((seq, LANES - MLA_ROPE), F32)
    return (jnp.concatenate([cos, cos, z64], axis=1), jnp.concatenate([-sin, z32, z64], axis=1),
            jnp.concatenate([z32, sin, z64], axis=1))


def _pad_cols(a, width):
    return jnp.pad(a, [(0, 0)] * (a.ndim - 1) + [(0, width - a.shape[-1])])


def even_layer(x, batch, seq, g_mix, w_in, w_out, lb, hg_gain, qn_g, kvn_g, w_uq, w_ukv, q_gain, k_gain):
    T, D = x.shape
    hgw = 5 * HG_WIDTH
    w_hg = w_in[:, :hgw].astype(BF16)
    w_mla = _pad_cols(w_in[:, hgw:], 6 * LANES).astype(BF16)
    proj_hg, proj_mla = norm_matmul(x, g_mix, [w_hg, w_mla], [F32, F32], 256)

    proj3 = proj_hg.reshape(batch, seq, hgw)
    rows = 256
    o_f = hgrn_direction(proj3, lb[0:1], False, rows)
    o_hg = hgrn_direction(proj3, lb[1:2], True, rows, o_fwd=o_f, gain=hg_gain)

    wq = _pad_cols(w_uq.reshape(MLA_Q_RANK, MLA_HEADS, MLA_QK), MLA_QK_PAD).reshape(MLA_Q_RANK, -1).astype(BF16)
    wkv = w_ukv.astype(BF16)
    qg = _pad_cols(q_gain.reshape(1, MLA_QK), MLA_QK_PAD)
    kg = _pad_cols(k_gain.reshape(1, MLA_QK), MLA_QK_PAD)
    q, k, v = mla_prep(proj_mla, qn_g.reshape(1, -1), kvn_g.reshape(1, -1), wq, wkv, qg, kg,
                       _rope_tables(seq), seq, 512)
    hq = MLA_HEADS * MLA_QK_PAD
    tq, tk = min(seq, 1024), min(seq, 2048)
    o_mla = attention(q.reshape(batch, seq, hq), k.reshape(batch, seq, hq),
                      v.reshape(batch, seq, hq), tq, tk)

    w_o = w_out.astype(BF16)
    return matmul_resid(x, [o_hg.reshape(T, HG_WIDTH), o_mla.reshape(T, MLA_HEADS * MLA_V)],
                        [w_o[:HG_WIDTH], w_o[HG_WIDTH:]], 512)


def odd_layer(x, batch, seq, g_mix, w_in, conv_w, conv_b, wa, ba, wx, bx, lam, w_out):
    T, D = x.shape
    (proj,) = norm_matmul(x, g_mix, [w_in.astype(BF16)], [F32], 256)
    proj3 = proj.reshape(batch, seq, 2 * D)
    sp = jax.nn.softplus(-lam.astype(F32))
    cb = conv_b.reshape(1, D)
    ts = 128
    h_f = lru_direction(proj3, conv_w, cb, wa[0].astype(BF16), ba[0:1], wx[0].astype(BF16), bx[0:1], sp[0:1],
                        False, ts)
    y = lru_direction(proj3, conv_w, cb, wa[1].astype(BF16), ba[1:2], wx[1].astype(BF16), bx[1:2], sp[1:2],
                      True, ts, h_fwd=h_f)
    return matmul_resid(x, [y.reshape(T, D)], [w_out.astype(BF16)], 512)


def _trunk(x3, norm_mix, norm_ffn, ev_w_in, ev_w_out, hg_lb_logits, hg_out_gain, mla_q_norm, mla_kv_norm,
           mla_w_uq, mla_w_ukv, mla_q_gain, mla_k_gain, od_w_in, od_conv_w, od_conv_b, rg_w_a, rg_b_a,
           rg_w_x, rg_b_x, rg_lambda, od_w_out, moe_router, moe_w_gate, moe_w_up, moe_w_down):
    batch, seq, D = x3.shape
    x = x3.reshape(batch * seq, D)
    lb_all = jnp.cumsum(jax.nn.softmax(hg_lb_logits.astype(F32), axis=1), axis=1)
    depth = norm_mix.shape[0]
    for layer in range(depth):
        j = layer // 2
        if layer % 2 == 0:
            x = even_layer(x, batch, seq, norm_mix[layer], ev_w_in[j], ev_w_out[j], lb_all[:, j],
                           hg_out_gain[j], mla_q_norm[j], mla_kv_norm[j], mla_w_uq[j], mla_w_ukv[j],
                           mla_q_gain[j], mla_k_gain[j])
        else:
            x = odd_layer(x, batch, seq, norm_mix[layer], od_w_in[j], od_conv_w[j], od_conv_b[j], rg_w_a[j],
                          rg_b_a[j], rg_w_x[j], rg_b_x[j], rg_lambda[j], od_w_out[j])
        x = expert_choice_moe(x, norm_ffn[layer], moe_router[layer].T, moe_w_gate[layer].astype(BF16),
                              moe_w_up[layer].astype(BF16), moe_w_down[layer].astype(BF16))
    return x.reshape(batch, seq, D)


def kernel(x_prompt, x_sample, norm_mix, norm_ffn, ev_w_in, ev_w_out, hg_lb_logits, hg_out_gain, mla_q_norm,
           mla_kv_norm, mla_w_uq, mla_w_ukv, mla_q_gain, mla_k_gain, od_w_in, od_conv_w, od_conv_b, rg_w_a,
           rg_b_a, rg_w_x, rg_b_x, rg_lambda, od_w_out, moe_router, moe_w_gate, moe_w_up, moe_w_down):
    params = (norm_mix, norm_ffn, ev_w_in, ev_w_out, hg_lb_logits, hg_out_gain, mla_q_norm, mla_kv_norm,
              mla_w_uq, mla_w_ukv, mla_q_gain, mla_k_gain, od_w_in, od_conv_w, od_conv_b, rg_w_a, rg_b_a,
              rg_w_x, rg_b_x, rg_lambda, od_w_out, moe_router, moe_w_gate, moe_w_up, moe_w_down)
    return (_trunk(x_prompt, *params), _trunk(x_sample, *params))
```

```python
import functools

import jax
import jax.numpy as jnp
from jax import lax
from jax.experimental import pallas as pl
from jax.experimental.pallas import tpu as pltpu

F32 = jnp.float32
BF16 = jnp.bfloat16
I32 = jnp.int32
EPS = 1e-6
HIGHEST = lax.Precision.HIGHEST
NT_DIMS = (((1,), (1,)), ((), ()))
TN_DIMS = (((0,), (0,)), ((), ()))

D_MODEL = 1024
HG_HEADS = 4
HG_DK = 128
HG_WIDTH = 512
HG_CHUNK = 64
HG_SUB = 16
HG_EXP_CLAMP = 80.0
MLA_HEADS = 4
MLA_Q_RANK = 384
MLA_KV_RANK = 256
MLA_NOPE = 128
MLA_ROPE = 64
MLA_V = 128
MLA_QK = MLA_NOPE + MLA_ROPE
MLA_QK_PAD = 256
ROPE_THETA = 10000.0
LRU_BLOCKS = 4
LRU_BW = 256
LRU_C = 8.0
N_EXPERTS = 16
CAPACITY = 2
LANES = 128
LOG2_E = 1.4426950408889634
MIB = 1024 * 1024


def _cparams(semantics, vmem_mib):
    return pltpu.CompilerParams(dimension_semantics=semantics, vmem_limit_bytes=vmem_mib * MIB)


def _sigmoid(x):
    return 1.0 / (1.0 + jnp.exp(-x))


def _rms_scale(x, width):
    return lax.rsqrt(jnp.sum(x * x, axis=-1, keepdims=True) * (1.0 / width) + EPS)


def _norm_matmul_kernel(x_ref, g_ref, *refs, n_out):
    x = x_ref[...]
    h = (x * _rms_scale(x, x.shape[-1]) * g_ref[...]).astype(BF16)
    for w_ref, o_ref in zip(refs[:n_out], refs[n_out:]):
        o_ref[...] = jnp.dot(h, w_ref[...], preferred_element_type=F32).astype(o_ref.dtype)


def norm_matmul(x, g, ws, out_dtypes, tm):
    T, D = x.shape
    n = len(ws)
    in_specs = [pl.BlockSpec((tm, D), lambda i: (i, 0)), pl.BlockSpec((1, D), lambda i: (0, 0))]
    in_specs += [pl.BlockSpec(w.shape, lambda i: (0, 0)) for w in ws]
    out_specs = [pl.BlockSpec((tm, w.shape[1]), lambda i: (i, 0)) for w in ws]
    out_shape = [jax.ShapeDtypeStruct((T, w.shape[1]), dt) for w, dt in zip(ws, out_dtypes)]
    return pl.pallas_call(
        functools.partial(_norm_matmul_kernel, n_out=n),
        grid=(T // tm,), in_specs=in_specs, out_specs=out_specs, out_shape=out_shape,
        compiler_params=_cparams(("parallel",), 56), name="norm_matmul",
    )(x, g.reshape(1, D), *ws)


def _matmul_resid_kernel(r_ref, *refs, n_in):
    acc = r_ref[...]
    for a_ref, w_ref in zip(refs[:n_in], refs[n_in:2 * n_in]):
        acc = acc + jnp.dot(a_ref[...], w_ref[...], preferred_element_type=F32)
    refs[2 * n_in][...] = acc


def matmul_resid(resid, acts, ws, tm):
    T, D = resid.shape
    n = len(acts)
    in_specs = [pl.BlockSpec((tm, D), lambda i: (i, 0))]
    in_specs += [pl.BlockSpec((tm, a.shape[1]), lambda i: (i, 0)) for a in acts]
    in_specs += [pl.BlockSpec(w.shape, lambda i: (0, 0)) for w in ws]
    return pl.pallas_call(
        functools.partial(_matmul_resid_kernel, n_in=n),
        grid=(T // tm,), in_specs=in_specs, out_specs=pl.BlockSpec((tm, D), lambda i: (i, 0)),
        out_shape=jax.ShapeDtypeStruct((T, D), F32),
        compiler_params=_cparams(("parallel",), 40), name="matmul_resid",
    )(resid, *acts, *ws)


def _hgrn_kernel(*refs, reverse, final, n_chunk):
    if final:
        q_ref, v_ref, fz_ref, lb_ref, g_ref, of_ref, gain_ref, o_ref, st_ref = refs
    else:
        q_ref, v_ref, fz_ref, lb_ref, o_ref, st_ref = refs
    C, SB = HG_CHUNK, HG_SUB

    @pl.when(pl.program_id(1) == 0)
    def _():
        st_ref[...] = jnp.zeros_like(st_ref)

    row = lax.broadcasted_iota(I32, (C, C), 0)
    col = lax.broadcasted_iota(I32, (C, C), 1)
    blk_of_row = row // SB
    if reverse:
        tri = col >= row
        ref_of_row = blk_of_row * SB
        causal = col >= row
    else:
        tri = col <= row
        ref_of_row = blk_of_row * SB + (SB - 1)
        causal = col <= row
    ref_sum = (col >= ref_of_row) if reverse else (col <= ref_of_row)
    cum_mat = jnp.concatenate([jnp.where(tri, 1.0, 0.0), jnp.where(ref_sum, 1.0, 0.0)], axis=0)
    cum_mat3 = jnp.concatenate([cum_mat, cum_mat, cum_mat], axis=1).astype(BF16)
    row_blk = lax.broadcasted_iota(I32, (C, HG_WIDTH), 0) // SB

    heads = [slice(h * HG_DK, (h + 1) * HG_DK) for h in range(HG_HEADS)]
    order = [((n_chunk - 1 - ci) if reverse else ci) for ci in range(n_chunk)]
    lb = lb_ref[...]

    per_chunk = []
    for c in order:
        rows = slice(c * C, (c + 1) * C)
        q = q_ref[rows, :] * (HG_DK ** -0.5)
        f = lb + (1.0 - lb) * _sigmoid(fz_ref[rows, :])
        logf = jnp.log(f)
        k = 1.0 - f
        hi = logf.astype(BF16)
        r1 = logf - hi.astype(F32)
        mid = r1.astype(BF16)
        lo = (r1 - mid.astype(F32)).astype(BF16)
        bb = jnp.dot(cum_mat3, jnp.concatenate([hi, mid, lo], axis=0), preferred_element_type=F32)
        b, b_blk = bb[:C], bb[C:]
        kt = k * jnp.exp(b_blk - b)
        q_parts, k_parts = [], []
        for j in range(C // SB):
            jr = j * SB if reverse else j * SB + SB - 1
            q_parts.append((q * jnp.exp(jnp.minimum(b - b[jr:jr + 1, :], HG_EXP_CLAMP))).astype(BF16))
            k_parts.append(jnp.where(row_blk == j, kt, 0.0).astype(BF16))
        b_end = b[0:1, :] if reverse else b[C - 1:C, :]
        vb = v_ref[rows, :].astype(BF16)
        intra = []
        for cs in heads:
            scores = lax.dot_general(jnp.concatenate([p[:, cs] for p in q_parts], axis=1),
                                     jnp.concatenate([p[:, cs] for p in k_parts], axis=1),
                                     NT_DIMS, preferred_element_type=F32)
            scores = jnp.where(causal, scores, 0.0).astype(BF16)
            intra.append(jnp.dot(scores, vb[:, cs], preferred_element_type=F32))
        per_chunk.append((rows, intra, (q * jnp.exp(b)).astype(BF16), (k * jnp.exp(b_end - b)).astype(BF16),
                          jnp.exp(b_end), vb))

    states = [st_ref[h] for h in range(HG_HEADS)]
    for rows, intra, q_dec, k_hat, decay_end, vb in per_chunk:
        for h, cs in enumerate(heads):
            o = intra[h] + lax.dot_general(q_dec[:, cs], states[h].astype(BF16), NT_DIMS,
                                           preferred_element_type=F32)
            states[h] = states[h] * decay_end[:, cs] + lax.dot_general(
                vb[:, cs], k_hat[:, cs], TN_DIMS, preferred_element_type=F32)
            if final:
                o = o + of_ref[rows, cs]
                y = o * _rms_scale(o, HG_DK) * gain_ref[...]
                g = g_ref[rows, cs]
                o_ref[rows, cs] = (y * (g * _sigmoid(g))).astype(o_ref.dtype)
            else:
                o_ref[rows, cs] = o
    for h in range(HG_HEADS):
        st_ref[h] = states[h]


def hgrn_direction(proj, lb, reverse, rows, o_fwd=None, gain=None):
    B, S, _ = proj.shape
    W = HG_WIDTH
    nblk = S // rows
    final = o_fwd is not None

    def rowmap(colblk):
        if reverse:
            return lambda b, c: (b, nblk - 1 - c, colblk)
        return lambda b, c: (b, c, colblk)

    blk = (None, rows, W)
    in_specs = [pl.BlockSpec(blk, rowmap(0)), pl.BlockSpec(blk, rowmap(1)),
                pl.BlockSpec(blk, rowmap(4 if reverse else 3)), pl.BlockSpec((1, W), lambda b, c: (0, 0))]
    args = [proj, proj, proj, lb]
    if final:
        in_specs += [pl.BlockSpec(blk, rowmap(2)), pl.BlockSpec(blk, rowmap(0)),
                     pl.BlockSpec((1, HG_DK), lambda b, c: (0, 0))]
        args += [proj, o_fwd, gain.reshape(1, HG_DK)]
    return pl.pallas_call(
        functools.partial(_hgrn_kernel, reverse=reverse, final=final, n_chunk=rows // HG_CHUNK),
        grid=(B, nblk), in_specs=in_specs, out_specs=pl.BlockSpec(blk, rowmap(0)),
        out_shape=jax.ShapeDtypeStruct((B, S, W), BF16 if final else F32),
        scratch_shapes=[pltpu.VMEM((HG_HEADS, HG_DK, HG_DK), F32)],
        compiler_params=_cparams(("parallel", "arbitrary"), 40),
        name="hgrn_bwd" if reverse else "hgrn_fwd",
    )(*args)


def _rope(x, cos_t, sin_a, sin_b):
    return x * cos_t + pltpu.roll(x, LANES - MLA_ROPE // 2, 1) * sin_a + pltpu.roll(x, MLA_ROPE // 2, 1) * sin_b


def _mla_prep_kernel(c_ref, qn_ref, kvn_ref, wq_ref, wkv_ref, qg_ref, kg_ref, cos_ref, sa_ref, sb_ref,
                     q_ref, k_ref, v_ref):
    c = c_ref[...]
    cq = c[:, :MLA_Q_RANK]
    ckv = c[:, MLA_Q_RANK:MLA_Q_RANK + MLA_KV_RANK]
    kpe = c[:, MLA_Q_RANK + MLA_KV_RANK:]
    qn = (cq * _rms_scale(cq, MLA_Q_RANK) * qn_ref[...]).astype(BF16)
    kvn = (ckv * _rms_scale(ckv, MLA_KV_RANK) * kvn_ref[...]).astype(BF16)
    qf = jnp.dot(qn, wq_ref[...], preferred_element_type=F32)
    kvf = jnp.dot(kvn, wkv_ref[...], preferred_element_type=F32)
    cos_t, sin_a, sin_b = cos_ref[...], sa_ref[...], sb_ref[...]
    qg, kg = qg_ref[...], kg_ref[...]
    sm_scale = (MLA_QK ** -0.5) * LOG2_E
    kpe_ss = jnp.sum(kpe * kpe, axis=-1, keepdims=True)
    for h in range(MLA_HEADS):
        lo = h * MLA_QK_PAD
        q_nope, q_rope = qf[:, lo:lo + LANES], qf[:, lo + LANES:lo + 2 * LANES]
        qs = lax.rsqrt((jnp.sum(q_nope * q_nope, axis=-1, keepdims=True)
                        + jnp.sum(q_rope * q_rope, axis=-1, keepdims=True)) * (1.0 / MLA_QK) + EPS)
        q_ref[:, lo:lo + LANES] = (q_nope * qs * qg[:, :LANES] * sm_scale).astype(BF16)
        q_ref[:, lo + LANES:lo + 2 * LANES] = (
            _rope(q_rope * qs * qg[:, LANES:], cos_t, sin_a, sin_b) * sm_scale).astype(BF16)
        k_nope = kvf[:, lo:lo + LANES]
        ks = lax.rsqrt((jnp.sum(k_nope * k_nope, axis=-1, keepdims=True) + kpe_ss) * (1.0 / MLA_QK) + EPS)
        k_ref[:, lo:lo + LANES] = (k_nope * ks * kg[:, :LANES]).astype(BF16)
        k_ref[:, lo + LANES:lo + 2 * LANES] = _rope(kpe * ks * kg[:, LANES:], cos_t, sin_a, sin_b).astype(BF16)
        v_ref[:, lo:lo + LANES] = kvf[:, lo + LANES:lo + 2 * LANES].astype(BF16)
        v_ref[:, lo + LANES:lo + 2 * LANES] = jnp.ones((c.shape[0], LANES), BF16)


def mla_prep(c, qn_g, kvn_g, wq, wkv, qg, kg, tables, seq, tm):
    T, CW = c.shape
    nseq = seq // tm
    full = lambda a: pl.BlockSpec(a.shape, lambda i: (0, 0))
    tab = pl.BlockSpec((tm, LANES), lambda i: (i % nseq, 0))
    HW = MLA_HEADS * MLA_QK_PAD
    return pl.pallas_call(
        _mla_prep_kernel, grid=(T // tm,),
        in_specs=[pl.BlockSpec((tm, CW), lambda i: (i, 0)), full(qn_g), full(kvn_g), full(wq), full(wkv),
                  full(qg), full(kg), tab, tab, tab],
        out_specs=[pl.BlockSpec((tm, HW), lambda i: (i, 0))] * 3,
        out_shape=[jax.ShapeDtypeStruct((T, HW), BF16)] * 3,
        compiler_params=_cparams(("parallel",), 40), name="mla_prep",
    )(c, qn_g, kvn_g, wq, wkv, qg, kg, *tables)


ATTN_PARTS = 4


def _attn_kernel(q_ref, k_ref, v_ref, o_ref, m_ref, l_ref, acc_ref):
    ki = pl.program_id(3)

    @pl.when(ki == 0)
    def _():
        m_ref[...] = jnp.full_like(m_ref, -jnp.inf)
        l_ref[...] = jnp.zeros_like(l_ref)
        acc_ref[...] = jnp.zeros_like(acc_ref)

    k, v = k_ref[...], v_ref[...]
    part = q_ref.shape[0] // ATTN_PARTS
    rows = [slice(i * part, (i + 1) * part) for i in range(ATTN_PARTS)]
    scores = lambda r: lax.dot_general(q_ref[r, :], k, NT_DIMS, preferred_element_type=F32)
    s_next = scores(rows[0])
    for i, r in enumerate(rows):
        s = s_next
        if i + 1 < ATTN_PARTS:
            s_next = scores(rows[i + 1])
        m_prev = m_ref[r, :]
        m_new = jnp.maximum(m_prev, jnp.max(s, axis=-1, keepdims=True))
        alpha = jnp.exp2(m_prev - m_new)
        p = jnp.exp2(s - m_new)
        l_ref[r, :] = alpha * l_ref[r, :] + jnp.sum(p, axis=-1, keepdims=True)
        acc_ref[r, :] = alpha * acc_ref[r, :] + jnp.dot(p.astype(BF16), v, preferred_element_type=F32)
        m_ref[r, :] = m_new

    @pl.when(ki == pl.num_programs(3) - 1)
    def _():
        o_ref[...] = (acc_ref[...] / l_ref[...]).astype(o_ref.dtype)


def attention(q, k, v, tq, tk):
    B, S, _ = q.shape
    return pl.pallas_call(
        _attn_kernel, grid=(B, MLA_HEADS, S // tq, S // tk),
        in_specs=[pl.BlockSpec((None, tq, MLA_QK_PAD), lambda b, h, i, j: (b, i, h)),
                  pl.BlockSpec((None, tk, MLA_QK_PAD), lambda b, h, i, j: (b, j, h)),
                  pl.BlockSpec((None, tk, MLA_V), lambda b, h, i, j: (b, j, 2 * h))],
        out_specs=pl.BlockSpec((None, tq, MLA_V), lambda b, h, i, j: (b, i, h)),
        out_shape=jax.ShapeDtypeStruct((B, S, MLA_HEADS * MLA_V), BF16),
        scratch_shapes=[pltpu.VMEM((tq, 1), F32), pltpu.VMEM((tq, 1), F32), pltpu.VMEM((tq, MLA_V), F32)],
        compiler_params=_cparams(("parallel", "parallel", "parallel", "arbitrary"), 48), name="mla_attention",
    )(q, k, v)


SUBLANES = 8


def _scan_rows(a, u, h_in, reverse):
    n_group = a.shape[0] // SUBLANES
    sub = lax.broadcasted_iota(I32, (SUBLANES, a.shape[1]), 0)
    scanned = []
    for g in range(n_group):
        rows = slice(g * SUBLANES, (g + 1) * SUBLANES)
        ag, ug = a[rows], u[rows]
        k = 1
        while k < SUBLANES:
            shift, valid = (SUBLANES - k, sub < SUBLANES - k) if reverse else (k, sub >= k)
            a_s, u_s = pltpu.roll(ag, shift, 0), pltpu.roll(ug, shift, 0)
            ug = ug + ag * jnp.where(valid, u_s, 0.0)
            ag = ag * jnp.where(valid, a_s, 1.0)
            k *= 2
        scanned.append((ag, ug))
    outs = [None] * n_group
    h = h_in
    for g in (range(n_group - 1, -1, -1) if reverse else range(n_group)):
        ag, ug = scanned[g]
        outs[g] = ug + ag * h
        h = outs[g][0:1] if reverse else outs[g][SUBLANES - 1:SUBLANES]
    return jnp.concatenate(outs, axis=0), h


def _lru_kernel(*refs, reverse, final):
    if final:
        (x_ref, prev_ref, next_ref, cw_ref, cb_ref, wa_ref, ba_ref, wx_ref, bx_ref, sp_ref,
         gate_ref, hf_ref, o_ref, carry_ref) = refs
    else:
        (x_ref, prev_ref, next_ref, cw_ref, cb_ref, wa_ref, ba_ref, wx_ref, bx_ref, sp_ref,
         o_ref, carry_ref) = refs
    step, nstep = pl.program_id(1), pl.num_programs(1)
    blk = (nstep - 1 - step) if reverse else step

    @pl.when(step == 0)
    def _():
        carry_ref[...] = jnp.zeros_like(carry_ref)

    x = x_ref[...]
    ts = x.shape[0]
    rowi = lax.broadcasted_iota(I32, x.shape, 0)
    has_prev = jnp.where(blk > 0, 1.0, 0.0)
    has_next = jnp.where(blk < nstep - 1, 1.0, 0.0)
    p6, p7, n0 = prev_ref[6:7, :] * has_prev, prev_ref[7:8, :] * has_prev, next_ref[0:1, :] * has_next
    xm1 = jnp.where(rowi == 0, p7, pltpu.roll(x, 1, 0))
    xm2 = jnp.where(rowi == 0, p6, jnp.where(rowi == 1, p7, pltpu.roll(x, 2, 0)))
    xp1 = jnp.where(rowi == ts - 1, n0, pltpu.roll(x, ts - 1, 0))
    cw = cw_ref[...]
    xc = xm2 * cw[0:1, :] + xm1 * cw[1:2, :] + x * cw[2:3, :] + xp1 * cw[3:4, :] + cb_ref[...]
    xcb = xc.astype(BF16)

    def block_diag(w_ref, b_ref):
        outs = [jnp.dot(xcb[:, n * LRU_BW:(n + 1) * LRU_BW], w_ref[n], preferred_element_type=F32)
                for n in range(LRU_BLOCKS)]
        return jnp.concatenate(outs, axis=1) + b_ref[...]

    r = _sigmoid(block_diag(wa_ref, ba_ref))
    i = _sigmoid(block_diag(wx_ref, bx_ref))
    log_a = (-LRU_C) * r * sp_ref[...]
    a = jnp.exp(log_a)
    u = jnp.sqrt(1.0 - a * a) * (i * xc)
    h, carry_ref[...] = _scan_rows(a, u, carry_ref[...], reverse)
    if final:
        g = gate_ref[...]
        gelu = 0.5 * g * (1.0 + jnp.tanh(0.7978845608028654 * (g + 0.044715 * (g * g * g))))
        o_ref[...] = ((h + hf_ref[...]) * gelu).astype(o_ref.dtype)
    else:
        o_ref[...] = h


def lru_direction(proj, cw, cb, wa, ba, wx, bx, sp, reverse, ts, h_fwd=None):
    B, S, W2 = proj.shape
    W = W2 // 2
    nblk = S // ts
    final = h_fwd is not None
    sub = ts // 8

    def pos(c):
        return (nblk - 1 - c) if reverse else c

    tile = (None, ts, W)
    halo = (None, 8, W)
    vec = lambda r: pl.BlockSpec((r, W), lambda b, c: (0, 0))
    wspec = pl.BlockSpec((LRU_BLOCKS, LRU_BW, LRU_BW), lambda b, c: (0, 0, 0))
    in_specs = [pl.BlockSpec(tile, lambda b, c: (b, pos(c), 1)),
                pl.BlockSpec(halo, lambda b, c: (b, jnp.maximum(pos(c) * sub - 1, 0), 1)),
                pl.BlockSpec(halo, lambda b, c: (b, jnp.minimum((pos(c) + 1) * sub, S // 8 - 1), 1)),
                vec(4), vec(1), wspec, vec(1), wspec, vec(1), vec(1)]
    args = [proj, proj, proj, cw, cb, wa, ba, wx, bx, sp]
    if final:
        in_specs += [pl.BlockSpec(tile, lambda b, c: (b, pos(c), 0)), pl.BlockSpec(tile, lambda b, c: (b, pos(c), 0))]
        args += [proj, h_fwd]
    return pl.pallas_call(
        functools.partial(_lru_kernel, reverse=reverse, final=final),
        grid=(B, nblk), in_specs=in_specs, out_specs=pl.BlockSpec(tile, lambda b, c: (b, pos(c), 0)),
        out_shape=jax.ShapeDtypeStruct((B, S, W), BF16 if final else F32),
        scratch_shapes=[pltpu.VMEM((1, W), F32)],
        compiler_params=_cparams(("parallel", "arbitrary"), 48),
        name="lru_bwd" if reverse else "lru_fwd",
    )(*args)


def _router_kernel(x_ref, g_ref, wr_ref, aff_ref, ext_ref):
    x = x_ref[...]
    d = x.shape[1]
    xn = x * _rms_scale(x, d) * g_ref[...]
    wr = wr_ref[...]
    x_hi, w_hi = xn.astype(BF16), wr.astype(BF16)
    x_lo, w_lo = (xn - x_hi.astype(F32)).astype(BF16), (wr - w_hi.astype(F32)).astype(BF16)
    logits = lax.dot_general(jnp.concatenate([w_hi, w_hi, w_lo], axis=1), jnp.concatenate([x_hi, x_lo, x_hi], axis=1),
                             NT_DIMS, preferred_element_type=F32)
    e = jnp.exp(logits - jnp.max(logits, axis=0, keepdims=True))
    aff = e / jnp.sum(e, axis=0, keepdims=True)
    aff_ref[...] = aff
    half = d // 2
    as_bits = lambda v: pltpu.bitcast(v.astype(BF16).astype(F32), I32)
    ext_ref[:, :half] = lax.shift_right_logical(as_bits(xn[:, :half]), jnp.int32(16)) | as_bits(xn[:, half:])
    aff_rows = jnp.concatenate([aff, jnp.zeros((LANES - aff.shape[0], aff.shape[1]), F32)], axis=0).T
    ext_ref[:, half:] = pltpu.bitcast(aff_rows, I32)


def router(x, g, wr_t, tm):
    N, D = x.shape
    E = wr_t.shape[0]
    return pl.pallas_call(
        _router_kernel, grid=(N // tm,),
        in_specs=[pl.BlockSpec((tm, D), lambda i: (i, 0)), pl.BlockSpec((1, D), lambda i: (0, 0)),
                  pl.BlockSpec((E, D), lambda i: (0, 0))],
        out_specs=[pl.BlockSpec((E, tm), lambda i: (0, i)), pl.BlockSpec((tm, D // 2 + LANES), lambda i: (i, 0))],
        out_shape=[jax.ShapeDtypeStruct((E, N), F32), jax.ShapeDtypeStruct((N, D // 2 + LANES), I32)],
        compiler_params=_cparams(("parallel",), 32), name="moe_router",
    )(x, g.reshape(1, D), wr_t)


def _topk_kernel(aff_ref, pos_ref, off_ref, cnt_ref, *, cap):
    bits = pltpu.bitcast(aff_ref[...], I32)
    nb = bits.shape[0]

    def count(pred):
        return jnp.sum(jnp.sum(jnp.where(pred, 1.0, 0.0), axis=1, keepdims=True), axis=0, keepdims=True)

    def bisect(i, lo):
        cand = lo | lax.shift_left(jnp.int32(1), jnp.int32(30) - i)
        return jnp.where(count(bits >= cand) >= cap, cand, lo)

    thr = lax.fori_loop(0, 31, bisect, jnp.zeros((1, 1), I32))

    r128 = lax.broadcasted_iota(I32, (LANES, LANES), 0)
    c128 = lax.broadcasted_iota(I32, (LANES, LANES), 1)
    upper = jnp.where(r128 <= c128, 1.0, 0.0).astype(BF16)
    rn = lax.broadcasted_iota(I32, (nb, nb), 0)
    cn = lax.broadcasted_iota(I32, (nb, nb), 1)
    strict_lower = jnp.where(cn < rn, 1.0, 0.0).astype(BF16)

    def running_count(m):
        incl = jnp.dot(m.astype(BF16), upper, preferred_element_type=F32)
        tot = jnp.broadcast_to(incl[:, LANES - 1:LANES], (nb, LANES)).astype(BF16)
        return incl, jnp.dot(strict_lower, tot, preferred_element_type=F32)

    gt = jnp.where(bits > thr, 1.0, 0.0)
    eq = jnp.where(bits == thr, 1.0, 0.0)
    need = cap - count(bits > thr)
    eq_incl, eq_off = running_count(eq)
    sel = gt + eq * jnp.where(eq_incl - eq + eq_off < need, 1.0, 0.0)
    incl, off = running_count(sel)
    cnt = incl + off
    pos_ref[...] = jnp.where(sel > 0.0, cnt - 1.0, -1.0).astype(I32)
    off_ref[...] = off.astype(I32)
    cnt_ref[...] = cnt


def topk_slots(aff3, cap):
    E, NB, _ = aff3.shape
    spec = pl.BlockSpec((None, NB, LANES), lambda e: (e, 0, 0))
    return pl.pallas_call(
        functools.partial(_topk_kernel, cap=cap), grid=(E,), in_specs=[spec], out_specs=[spec, spec, spec],
        out_shape=[jax.ShapeDtypeStruct((E, NB, LANES), I32), jax.ShapeDtypeStruct((E, NB, LANES), I32),
                   jax.ShapeDtypeStruct((E, NB, LANES), F32)],
        compiler_params=_cparams(("parallel",), 32), name="moe_topk",
    )(aff3)


def _compact_kernel(blo_ref, bhi_ref, cnt_ref, idx_ref, *, n_chunk):
    e = pl.program_id(0)
    slot = lax.broadcasted_iota(I32, (LANES, LANES), 0)

    def chunk(c, carry):
        lo, hi = blo_ref[e, c], bhi_ref[e, c]
        sigma = (slot + c * LANES).astype(F32)

        def blk(b, acc):
            return acc + jnp.where(cnt_ref[b] <= sigma, 1.0, 0.0)

        acc = lax.fori_loop(lo, hi + 1, blk, jnp.zeros((LANES, LANES), F32))
        col = jnp.sum(acc, axis=1, keepdims=True) + (lo * LANES).astype(F32)
        idx_ref[c] = jnp.broadcast_to(col, (LANES, LANES)).T[0:1, :].astype(I32)
        return carry

    lax.fori_loop(0, n_chunk, chunk, 0)


def compact_slots(cnt3, cap):
    E, NB, _ = cnt3.shape
    n_chunk = cap // LANES
    starts = jnp.arange(n_chunk, dtype=F32) * LANES
    blo = jnp.sum(cnt3[:, :, LANES - 1, None] <= starts[None, None, :], axis=1).astype(I32)
    bhi = jnp.sum(cnt3[:, :, 0, None] <= starts[None, None, :] + (LANES - 1), axis=1).astype(I32) - 1
    grid_spec = pltpu.PrefetchScalarGridSpec(
        num_scalar_prefetch=2, grid=(E,),
        in_specs=[pl.BlockSpec((None, NB, 1, LANES), lambda e, *_: (e, 0, 0, 0))],
        out_specs=pl.BlockSpec((None, n_chunk, 1, LANES), lambda e, *_: (e, 0, 0, 0)))
    idx = pl.pallas_call(
        functools.partial(_compact_kernel, n_chunk=n_chunk), grid_spec=grid_spec,
        out_shape=jax.ShapeDtypeStruct((E, n_chunk, 1, LANES), I32),
        compiler_params=_cparams(("arbitrary",), 32), name="moe_compact",
    )(blo, bhi, cnt3.reshape(E, NB, 1, LANES))
    return idx.reshape(E, cap)


def _ffn_kernel(idx0_ref, idx1_ref, idx2_ref, ext_hbm, wg_ref, wu_ref, wd_ref, o_ref, rows_a, rows_b, sem,
                *, steps_per_expert):
    t, n_steps = pl.program_id(0), pl.num_programs(0)
    tm, d = rows_a.shape[0], o_ref.shape[1]

    def row_copy(ids_ref, rows_ref, s, r):
        return pltpu.make_async_copy(ext_hbm.at[pl.ds(ids_ref[0, 0, r], 1)], rows_ref.at[pl.ds(r, 1)], sem.at[s])

    def start_rows(ids_ref, rows_ref, s):
        for r in range(tm):
            row_copy(ids_ref, rows_ref, s, r).start(priority=r % 2)

    def wait_rows(rows_ref, s):
        pltpu.make_async_copy(ext_hbm.at[pl.ds(0, tm)], rows_ref, sem.at[s]).wait()

    def ffn(rows_ref, out_rows):
        ext = rows_ref[...]
        half = d // 2
        words = ext[:, :half]
        xn = jnp.concatenate([pltpu.bitcast(lax.shift_left(words, jnp.int32(16)), F32),
                              pltpu.bitcast(words & jnp.int32(-65536), F32)], axis=1).astype(BF16)
        lane = lax.broadcasted_iota(I32, (tm, LANES), 1)
        route_gate = jnp.sum(jnp.where(lane == t // steps_per_expert, pltpu.bitcast(ext[:, half:], F32), 0.0),
                             axis=1, keepdims=True)
        gate = jnp.dot(xn, wg_ref[...], preferred_element_type=F32)
        up = jnp.dot(xn, wu_ref[...], preferred_element_type=F32)
        hid = (gate * _sigmoid(gate) * up).astype(BF16)
        o_ref[out_rows, :] = (jnp.dot(hid, wd_ref[...], preferred_element_type=F32) * route_gate).astype(o_ref.dtype)

    @pl.when(t == 0)
    def _():
        def first(r, carry):
            row_copy(idx0_ref, rows_a, 0, r).start()
            return carry
        lax.fori_loop(0, tm, first, 0)

    wait_rows(rows_a, 0)
    start_rows(idx1_ref, rows_b, 1)
    ffn(rows_a, slice(0, tm))
    wait_rows(rows_b, 1)
    start_rows(idx2_ref, rows_a, 0)
    ffn(rows_b, slice(tm, 2 * tm))

    @pl.when(t == n_steps - 1)
    def _():
        wait_rows(rows_a, 0)


def expert_ffn(idx, ext, wg, wu, wd, tm):
    E, cap = idx.shape
    D, FF = wg.shape[1], wg.shape[2]
    steps_per_expert = cap // (2 * tm)
    n_tiles = E * cap // tm
    idx3 = idx.reshape(n_tiles, 1, tm)
    smem = lambda imap: pl.BlockSpec((1, 1, tm), imap, memory_space=pltpu.SMEM)
    wspec = lambda shape: pl.BlockSpec((None,) + shape, lambda t: (t // steps_per_expert, 0, 0))
    ye = pl.pallas_call(
        functools.partial(_ffn_kernel, steps_per_expert=steps_per_expert), grid=(n_tiles // 2,),
        in_specs=[smem(lambda t: (2 * t, 0, 0)), smem(lambda t: (2 * t + 1, 0, 0)),
                  smem(lambda t: (jnp.minimum(2 * t + 2, n_tiles - 1), 0, 0)),
                  pl.BlockSpec(memory_space=pl.ANY),
                  wspec((D, FF)), wspec((D, FF)), wspec((FF, D))],
        out_specs=pl.BlockSpec((2 * tm, D), lambda t: (t, 0)),
        out_shape=jax.ShapeDtypeStruct((E * cap, D), BF16),
        scratch_shapes=[pltpu.VMEM((tm, ext.shape[1]), I32), pltpu.VMEM((tm, ext.shape[1]), I32),
                        pltpu.SemaphoreType.DMA((2,))],
        compiler_params=_cparams(("arbitrary",), 56), name="moe_ffn",
    )(idx3, idx3, idx3, ext, wg, wu, wd)
    return ye.reshape(E, cap, D)


COMBINE_TOKENS = 128


COMBINE_ALIGN = 16
COMBINE_SMALL = 48


def _combine_kernel(s0_small_ref, s0_full_ref, needs_full_ref, x_ref, pos_ref, ye_hbm, o_ref,
                    small_ref, full_ref, sem_small, sem_full):
    t, n_tiles = pl.program_id(0), pl.num_programs(0)
    n_exp = pos_ref.shape[1]
    small, full = small_ref.shape[1] // n_exp, full_ref.shape[1]
    par = t % 2

    def small_window(tile, e, p):
        s0 = pl.multiple_of(s0_small_ref[e, tile], COMBINE_ALIGN)
        return pltpu.make_async_copy(ye_hbm.at[e, pl.ds(s0, small)], small_ref.at[p, pl.ds(e * small, small)],
                                     sem_small.at[p, e])

    def full_window(e):
        s0 = pl.multiple_of(s0_full_ref[e, t], COMBINE_ALIGN)
        return pltpu.make_async_copy(ye_hbm.at[e, pl.ds(s0, full)], full_ref.at[e], sem_full.at[e])

    def scatter_sum(s0_ref, win, rows_of):
        lane = lax.broadcasted_iota(I32, (x_ref.shape[0], win), 1)
        onehots = [jnp.where(pos_ref[:, e:e + 1] == lane + s0_ref[e, t], 1.0, 0.0).astype(BF16)
                   for e in range(n_exp)]
        parts = [jnp.dot(onehots[e], rows_of(e), preferred_element_type=F32) for e in range(n_exp)]
        while len(parts) > 1:
            parts = [a + b for a, b in zip(parts[0::2], parts[1::2])]
        return x_ref[...] + parts[0]

    @pl.when(t == 0)
    def _():
        for e in range(n_exp):
            small_window(0, e, 0).start()

    @pl.when(t + 1 < n_tiles)
    def _():
        for e in range(n_exp):
            small_window(t + 1, e, 1 - par).start(priority=e % 2)

    for e in range(n_exp):
        small_window(t, e, par).wait()

    @pl.when(needs_full_ref[t] == 0)
    def _():
        width = n_exp * small
        col = lax.broadcasted_iota(I32, (1, width), 1)
        owner = col // small
        target = col - owner * small
        wanted = jnp.full((x_ref.shape[0], width), -2, I32)
        for e in range(n_exp):
            target = target + jnp.where(owner == e, s0_small_ref[e, t], 0)
            wanted = jnp.where(owner == e, pos_ref[:, e:e + 1], wanted)
        onehot = jnp.where(wanted == target, 1.0, 0.0).astype(BF16)
        o_ref[...] = x_ref[...] + jnp.dot(onehot, small_ref[par], preferred_element_type=F32)

    @pl.when(needs_full_ref[t] != 0)
    def _():
        for e in range(n_exp):
            full_window(e).start()
        for e in range(n_exp):
            full_window(e).wait()
        o_ref[...] = scatter_sum(s0_full_ref, full, lambda e: full_ref[e])


def moe_combine(x, pos_t, off, ye, tt):
    N, D = x.shape
    E, cap, _ = ye.shape
    full = tt + COMBINE_ALIGN
    small = min(COMBINE_SMALL, full)
    aligned = (off // COMBINE_ALIGN) * COMBINE_ALIGN
    s0_small = jnp.clip(aligned, 0, cap - small).astype(I32)
    s0_full = jnp.clip(aligned, 0, cap - full).astype(I32)
    end = jnp.concatenate([off[:, 1:], jnp.full((E, 1), cap, off.dtype)], axis=1)
    needs_full = jnp.any(end > s0_small + small, axis=0).astype(I32)
    grid_spec = pltpu.PrefetchScalarGridSpec(
        num_scalar_prefetch=3, grid=(N // tt,),
        in_specs=[pl.BlockSpec((tt, D), lambda t, *_: (t, 0)), pl.BlockSpec((tt, E), lambda t, *_: (t, 0)),
                  pl.BlockSpec(memory_space=pl.ANY)],
        out_specs=pl.BlockSpec((tt, D), lambda t, *_: (t, 0)),
        scratch_shapes=[pltpu.VMEM((2, E * small, D), BF16), pltpu.VMEM((E, full, D), BF16),
                        pltpu.SemaphoreType.DMA((2, E)), pltpu.SemaphoreType.DMA((E,))])
    return pl.pallas_call(
        _combine_kernel, grid_spec=grid_spec,
        out_shape=jax.ShapeDtypeStruct((N, D), F32),
        compiler_params=_cparams(("arbitrary",), 40), name="moe_combine",
    )(s0_small, s0_full, needs_full, x, pos_t, ye)


def expert_choice_moe(x, g, wr_t, wg, wu, wd):
    N, D = x.shape
    E = wr_t.shape[0]
    cap = CAPACITY * N // E
    nb = N // LANES
    aff, ext = router(x, g, wr_t, 512)
    pos3, off3, cnt3 = topk_slots(aff.reshape(E, nb, LANES), cap)
    idx = compact_slots(cnt3, cap)
    ye = expert_ffn(idx, ext, wg, wu, wd, min(512, cap // 2))
    return moe_combine(x, pos3.reshape(E, N).T, off3[:, ::COMBINE_TOKENS // LANES, 0], ye, COMBINE_TOKENS)


def _rope_tables(seq):
    pos = jnp.arange(seq, dtype=F32)
    inv = 1.0 / (ROPE_THETA ** (jnp.arange(0, MLA_ROPE, 2, dtype=F32) / MLA_ROPE))
    ang = pos[:, None] * inv[None, :]
    cos, sin = jnp.cos(ang), jnp.sin(ang)
    z32, z64 = jnp.zeros_like(cos), jnp.zeros((seq, LANES - MLA_ROPE), F32)
    return (jnp.concatenate([cos, cos, z64], axis=1), jnp.concatenate([-sin, z32, z64], axis=1),
            jnp.concatenate([z32, sin, z64], axis=1))


def _pad_cols(a, width):
    return jnp.pad(a, [(0, 0)] * (a.ndim - 1) + [(0, width - a.shape[-1])])


def even_layer(x, batch, seq, g_mix, w_in, w_out, lb, hg_gain, qn_g, kvn_g, w_uq, w_ukv, q_gain, k_gain):
    T, D = x.shape
    hgw = 5 * HG_WIDTH
    w_hg = w_in[:, :hgw].astype(BF16)
    w_mla = _pad_cols(w_in[:, hgw:], 6 * LANES).astype(BF16)
    proj_hg, proj_mla = norm_matmul(x, g_mix, [w_hg, w_mla], [F32, F32], 256)

    proj3 = proj_hg.reshape(batch, seq, hgw)
    rows = 256
    o_f = hgrn_direction(proj3, lb[0:1], False, rows)
    o_hg = hgrn_direction(proj3, lb[1:2], True, rows, o_fwd=o_f, gain=hg_gain)

    wq = _pad_cols(w_uq.reshape(MLA_Q_RANK, MLA_HEADS, MLA_QK), MLA_QK_PAD).reshape(MLA_Q_RANK, -1).astype(BF16)
    wkv = w_ukv.astype(BF16)
    qg = _pad_cols(q_gain.reshape(1, MLA_QK), MLA_QK_PAD)
    kg = _pad_cols(k_gain.reshape(1, MLA_QK), MLA_QK_PAD)
    q, k, v = mla_prep(proj_mla, qn_g.reshape(1, -1), kvn_g.reshape(1, -1), wq, wkv, qg, kg,
                       _rope_tables(seq), seq, 512)
    hq = MLA_HEADS * MLA_QK_PAD
    tq, tk = min(seq, 1024), min(seq, 2048)
    o_mla = attention(q.reshape(batch, seq, hq), k.reshape(batch, seq, hq),
                      v.reshape(batch, seq, hq), tq, tk)

    w_o = w_out.astype(BF16)
    return matmul_resid(x, [o_hg.reshape(T, HG_WIDTH), o_mla.reshape(T, MLA_HEADS * MLA_V)],
                        [w_o[:HG_WIDTH], w_o[HG_WIDTH:]], 512)


def odd_layer(x, batch, seq, g_mix, w_in, conv_w, conv_b, wa, ba, wx, bx, lam, w_out):
    T, D = x.shape
    (proj,) = norm_matmul(x, g_mix, [w_in.astype(BF16)], [F32], 256)
    proj3 = proj.reshape(batch, seq, 2 * D)
    sp = jax.nn.softplus(-lam.astype(F32))
    cb = conv_b.reshape(1, D)
    ts = 128
    h_f = lru_direction(proj3, conv_w, cb, wa[0].astype(BF16), ba[0:1], wx[0].astype(BF16), bx[0:1], sp[0:1],
                        False, ts)
    y = lru_direction(proj3, conv_w, cb, wa[1].astype(BF16), ba[1:2], wx[1].astype(BF16), bx[1:2], sp[1:2],
                      True, ts, h_fwd=h_f)
    return matmul_resid(x, [y.reshape(T, D)], [w_out.astype(BF16)], 512)


def _trunk(x3, norm_mix, norm_ffn, ev_w_in, ev_w_out, hg_lb_logits, hg_out_gain, mla_q_norm, mla_kv_norm,
           mla_w_uq, mla_w_ukv, mla_q_gain, mla_k_gain, od_w_in, od_conv_w, od_conv_b, rg_w_a, rg_b_a,
           rg_w_x, rg_b_x, rg_lambda, od_w_out, moe_router, moe_w_gate, moe_w_up, moe_w_down):
    batch, seq, D = x3.shape
    x = x3.reshape(batch * seq, D)
    lb_all = jnp.cumsum(jax.nn.softmax(hg_lb_logits.astype(F32), axis=1), axis=1)
    depth = norm_mix.shape[0]
    for layer in range(depth):
        j = layer // 2
        if layer % 2 == 0:
            x = even_layer(x, batch, seq, norm_mix[layer], ev_w_in[j], ev_w_out[j], lb_all[:, j],
                           hg_out_gain[j], mla_q_norm[j], mla_kv_norm[j], mla_w_uq[j], mla_w_ukv[j],
                           mla_q_gain[j], mla_k_gain[j])
        else:
            x = odd_layer(x, batch, seq, norm_mix[layer], od_w_in[j], od_conv_w[j], od_conv_b[j], rg_w_a[j],
                          rg_b_a[j], rg_w_x[j], rg_b_x[j], rg_lambda[j], od_w_out[j])
        x = expert_choice_moe(x, norm_ffn[layer], moe_router[layer].T, moe_w_gate[layer].astype(BF16),
                              moe_w_up[layer].astype(BF16), moe_w_down[layer].astype(BF16))
    return x.reshape(batch, seq, D)


def kernel(x_prompt, x_sample, norm_mix, norm_ffn, ev_w_in, ev_w_out, hg_lb_logits, hg_out_gain, mla_q_norm,
           mla_kv_norm, mla_w_uq, mla_w_ukv, mla_q_gain, mla_k_gain, od_w_in, od_conv_w, od_conv_b, rg_w_a,
           rg_b_a, rg_w_x, rg_b_x, rg_lambda, od_w_out, moe_router, moe_w_gate, moe_w_up, moe_w_down):
    params = (norm_mix, norm_ffn, ev_w_in, ev_w_out, hg_lb_logits, hg_out_gain, mla_q_norm, mla_kv_norm,
              mla_w_uq, mla_w_ukv, mla_q_gain, mla_k_gain, od_w_in, od_conv_w, od_conv_b, rg_w_a, rg_b_a,
              rg_w_x, rg_b_x, rg_lambda, od_w_out, moe_router, moe_w_gate, moe_w_up, moe_w_down)
    return (_trunk(x_prompt, *params), _trunk(x_sample, *params))
```

```python
import functools

import jax
import jax.numpy as jnp
from jax import lax
from jax.experimental import pallas as pl
from jax.experimental.pallas import tpu as pltpu

F32 = jnp.float32
BF16 = jnp.bfloat16
I32 = jnp.int32
EPS = 1e-6
HIGHEST = lax.Precision.HIGHEST
NT_DIMS = (((1,), (1,)), ((), ()))
TN_DIMS = (((0,), (0,)), ((), ()))

D_MODEL = 1024
HG_HEADS = 4
HG_DK = 128
HG_WIDTH = 512
HG_CHUNK = 64
HG_SUB = 16
HG_EXP_CLAMP = 80.0
MLA_HEADS = 4
MLA_Q_RANK = 384
MLA_KV_RANK = 256
MLA_NOPE = 128
MLA_ROPE = 64
MLA_V = 128
MLA_QK = MLA_NOPE + MLA_ROPE
MLA_QK_PAD = 256
ROPE_THETA = 10000.0
LRU_BLOCKS = 4
LRU_BW = 256
LRU_C = 8.0
N_EXPERTS = 16
CAPACITY = 2
LANES = 128
LOG2_E = 1.4426950408889634
MIB = 1024 * 1024


def _cparams(semantics, vmem_mib):
    return pltpu.CompilerParams(dimension_semantics=semantics, vmem_limit_bytes=vmem_mib * MIB)


def _sigmoid(x):
    return 1.0 / (1.0 + jnp.exp(-x))


def _rms_scale(x, width):
    return lax.rsqrt(jnp.sum(x * x, axis=-1, keepdims=True) * (1.0 / width) + EPS)


def _norm_matmul_kernel(x_ref, g_ref, *refs, n_out):
    x = x_ref[...]
    h = (x * _rms_scale(x, x.shape[-1]) * g_ref[...]).astype(BF16)
    for w_ref, o_ref in zip(refs[:n_out], refs[n_out:]):
        o_ref[...] = jnp.dot(h, w_ref[...], preferred_element_type=F32).astype(o_ref.dtype)


def norm_matmul(x, g, ws, out_dtypes, tm):
    T, D = x.shape
    n = len(ws)
    in_specs = [pl.BlockSpec((tm, D), lambda i: (i, 0)), pl.BlockSpec((1, D), lambda i: (0, 0))]
    in_specs += [pl.BlockSpec(w.shape, lambda i: (0, 0)) for w in ws]
    out_specs = [pl.BlockSpec((tm, w.shape[1]), lambda i: (i, 0)) for w in ws]
    out_shape = [jax.ShapeDtypeStruct((T, w.shape[1]), dt) for w, dt in zip(ws, out_dtypes)]
    return pl.pallas_call(
        functools.partial(_norm_matmul_kernel, n_out=n),
        grid=(T // tm,), in_specs=in_specs, out_specs=out_specs, out_shape=out_shape,
        compiler_params=_cparams(("parallel",), 56), name="norm_matmul",
    )(x, g.reshape(1, D), *ws)


def _matmul_resid_kernel(r_ref, *refs, n_in):
    acc = r_ref[...]
    for a_ref, w_ref in zip(refs[:n_in], refs[n_in:2 * n_in]):
        acc = acc + jnp.dot(a_ref[...], w_ref[...], preferred_element_type=F32)
    refs[2 * n_in][...] = acc


def matmul_resid(resid, acts, ws, tm):
    T, D = resid.shape
    n = len(acts)
    in_specs = [pl.BlockSpec((tm, D), lambda i: (i, 0))]
    in_specs += [pl.BlockSpec((tm, a.shape[1]), lambda i: (i, 0)) for a in acts]
    in_specs += [pl.BlockSpec(w.shape, lambda i: (0, 0)) for w in ws]
    return pl.pallas_call(
        functools.partial(_matmul_resid_kernel, n_in=n),
        grid=(T // tm,), in_specs=in_specs, out_specs=pl.BlockSpec((tm, D), lambda i: (i, 0)),
        out_shape=jax.ShapeDtypeStruct((T, D), F32),
        compiler_params=_cparams(("parallel",), 40), name="matmul_resid",
    )(resid, *acts, *ws)


def _hgrn_kernel(*refs, reverse, final, n_chunk):
    if final:
        q_ref, v_ref, fz_ref, lb_ref, g_ref, of_ref, gain_ref, o_ref, st_ref = refs
    else:
        q_ref, v_ref, fz_ref, lb_ref, o_ref, st_ref = refs
    C, SB = HG_CHUNK, HG_SUB

    @pl.when(pl.program_id(1) == 0)
    def _():
        st_ref[...] = jnp.zeros_like(st_ref)

    row = lax.broadcasted_iota(I32, (C, C), 0)
    col = lax.broadcasted_iota(I32, (C, C), 1)
    blk_of_row = row // SB
    if reverse:
        tri = col >= row
        ref_of_row = blk_of_row * SB
        causal = col >= row
    else:
        tri = col <= row
        ref_of_row = blk_of_row * SB + (SB - 1)
        causal = col <= row
    ref_sum = (col >= ref_of_row) if reverse else (col <= ref_of_row)
    cum_mat = jnp.concatenate([jnp.where(tri, 1.0, 0.0), jnp.where(ref_sum, 1.0, 0.0)], axis=0)
    cum_mat3 = jnp.concatenate([cum_mat, cum_mat, cum_mat], axis=1).astype(BF16)
    row_blk = lax.broadcasted_iota(I32, (C, HG_WIDTH), 0) // SB

    heads = [slice(h * HG_DK, (h + 1) * HG_DK) for h in range(HG_HEADS)]
    order = [((n_chunk - 1 - ci) if reverse else ci) for ci in range(n_chunk)]
    lb = lb_ref[...]

    per_chunk = []
    for c in order:
        rows = slice(c * C, (c + 1) * C)
        q = q_ref[rows, :] * (HG_DK ** -0.5)
        f = lb + (1.0 - lb) * _sigmoid(fz_ref[rows, :])
        logf = jnp.log(f)
        k = 1.0 - f
        hi = logf.astype(BF16)
        r1 = logf - hi.astype(F32)
        mid = r1.astype(BF16)
        lo = (r1 - mid.astype(F32)).astype(BF16)
        bb = jnp.dot(cum_mat3, jnp.concatenate([hi, mid, lo], axis=0), preferred_element_type=F32)
        b, b_blk = bb[:C], bb[C:]
        kt = k * jnp.exp(b_blk - b)
        q_parts, k_parts = [], []
        for j in range(C // SB):
            jr = j * SB if reverse else j * SB + SB - 1
            q_parts.append((q * jnp.exp(jnp.minimum(b - b[jr:jr + 1, :], HG_EXP_CLAMP))).astype(BF16))
            k_parts.append(jnp.where(row_blk == j, kt, 0.0).astype(BF16))
        b_end = b[0:1, :] if reverse else b[C - 1:C, :]
        vb = v_ref[rows, :].astype(BF16)
        intra = []
        for cs in heads:
            scores = lax.dot_general(jnp.concatenate([p[:, cs] for p in q_parts], axis=1),
                                     jnp.concatenate([p[:, cs] for p in k_parts], axis=1),
                                     NT_DIMS, preferred_element_type=F32)
            scores = jnp.where(causal, scores, 0.0).astype(BF16)
            intra.append(jnp.dot(scores, vb[:, cs], preferred_element_type=F32))
        per_chunk.append((rows, intra, (q * jnp.exp(b)).astype(BF16), (k * jnp.exp(b_end - b)).astype(BF16),
                          jnp.exp(b_end), vb))

    states = [st_ref[h] for h in range(HG_HEADS)]
    for rows, intra, q_dec, k_hat, decay_end, vb in per_chunk:
        for h, cs in enumerate(heads):
            o = intra[h] + lax.dot_general(q_dec[:, cs], states[h].astype(BF16), NT_DIMS,
                                           preferred_element_type=F32)
            states[h] = states[h] * decay_end[:, cs] + lax.dot_general(
                vb[:, cs], k_hat[:, cs], TN_DIMS, preferred_element_type=F32)
            if final:
                o = o + of_ref[rows, cs]
                y = o * _rms_scale(o, HG_DK) * gain_ref[...]
                g = g_ref[rows, cs]
                o_ref[rows, cs] = (y * (g * _sigmoid(g))).astype(o_ref.dtype)
            else:
                o_ref[rows, cs] = o
    for h in range(HG_HEADS):
        st_ref[h] = states[h]


def hgrn_direction(proj, lb, reverse, rows, o_fwd=None, gain=None):
    B, S, _ = proj.shape
    W = HG_WIDTH
    nblk = S // rows
    final = o_fwd is not None

    def rowmap(colblk):
        if reverse:
            return lambda b, c: (b, nblk - 1 - c, colblk)
        return lambda b, c: (b, c, colblk)

    blk = (None, rows, W)
    in_specs = [pl.BlockSpec(blk, rowmap(0)), pl.BlockSpec(blk, rowmap(1)),
                pl.BlockSpec(blk, rowmap(4 if reverse else 3)), pl.BlockSpec((1, W), lambda b, c: (0, 0))]
    args = [proj, proj, proj, lb]
    if final:
        in_specs += [pl.BlockSpec(blk, rowmap(2)), pl.BlockSpec(blk, rowmap(0)),
                     pl.BlockSpec((1, HG_DK), lambda b, c: (0, 0))]
        args += [proj, o_fwd, gain.reshape(1, HG_DK)]
    return pl.pallas_call(
        functools.partial(_hgrn_kernel, reverse=reverse, final=final, n_chunk=rows // HG_CHUNK),
        grid=(B, nblk), in_specs=in_specs, out_specs=pl.BlockSpec(blk, rowmap(0)),
        out_shape=jax.ShapeDtypeStruct((B, S, W), BF16 if final else F32),
        scratch_shapes=[pltpu.VMEM((HG_HEADS, HG_DK, HG_DK), F32)],
        compiler_params=_cparams(("parallel", "arbitrary"), 40),
        name="hgrn_bwd" if reverse else "hgrn_fwd",
    )(*args)


def _rope(x, cos_t, sin_a, sin_b):
    return x * cos_t + pltpu.roll(x, LANES - MLA_ROPE // 2, 1) * sin_a + pltpu.roll(x, MLA_ROPE // 2, 1) * sin_b


def _mla_prep_kernel(c_ref, qn_ref, kvn_ref, wq_ref, wkv_ref, qg_ref, kg_ref, cos_ref, sa_ref, sb_ref,
                     q_ref, k_ref, v_ref):
    c = c_ref[...]
    cq = c[:, :MLA_Q_RANK]
    ckv = c[:, MLA_Q_RANK:MLA_Q_RANK + MLA_KV_RANK]
    kpe = c[:, MLA_Q_RANK + MLA_KV_RANK:]
    qn = (cq * _rms_scale(cq, MLA_Q_RANK) * qn_ref[...]).astype(BF16)
    kvn = (ckv * _rms_scale(ckv, MLA_KV_RANK) * kvn_ref[...]).astype(BF16)
    qf = jnp.dot(qn, wq_ref[...], preferred_element_type=F32)
    kvf = jnp.dot(kvn, wkv_ref[...], preferred_element_type=F32)
    cos_t, sin_a, sin_b = cos_ref[...], sa_ref[...], sb_ref[...]
    qg, kg = qg_ref[...], kg_ref[...]
    sm_scale = (MLA_QK ** -0.5) * LOG2_E
    kpe_ss = jnp.sum(kpe * kpe, axis=-1, keepdims=True)
    for h in range(MLA_HEADS):
        lo = h * MLA_QK_PAD
        q_nope, q_rope = qf[:, lo:lo + LANES], qf[:, lo + LANES:lo + 2 * LANES]
        qs = lax.rsqrt((jnp.sum(q_nope * q_nope, axis=-1, keepdims=True)
                        + jnp.sum(q_rope * q_rope, axis=-1, keepdims=True)) * (1.0 / MLA_QK) + EPS)
        q_ref[:, lo:lo + LANES] = (q_nope * qs * qg[:, :LANES] * sm_scale).astype(BF16)
        q_ref[:, lo + LANES:lo + 2 * LANES] = (
            _rope(q_rope * qs * qg[:, LANES:], cos_t, sin_a, sin_b) * sm_scale).astype(BF16)
        k_nope = kvf[:, lo:lo + LANES]
        ks = lax.rsqrt((jnp.sum(k_nope * k_nope, axis=-1, keepdims=True) + kpe_ss) * (1.0 / MLA_QK) + EPS)
        k_ref[:, lo:lo + LANES] = (k_nope * ks * kg[:, :LANES]).astype(BF16)
        k_ref[:, lo + LANES:lo + 2 * LANES] = _rope(kpe * ks * kg[:, LANES:], cos_t, sin_a, sin_b).astype(BF16)
        v_ref[:, lo:lo + LANES] = kvf[:, lo + LANES:lo + 2 * LANES].astype(BF16)
        v_ref[:, lo + LANES:lo + 2 * LANES] = jnp.ones((c.shape[0], LANES), BF16)


def mla_prep(c, qn_g, kvn_g, wq, wkv, qg, kg, tables, seq, tm):
    T, CW = c.shape
    nseq = seq // tm
    full = lambda a: pl.BlockSpec(a.shape, lambda i: (0, 0))
    tab = pl.BlockSpec((tm, LANES), lambda i: (i % nseq, 0))
    HW = MLA_HEADS * MLA_QK_PAD
    return pl.pallas_call(
        _mla_prep_kernel, grid=(T // tm,),
        in_specs=[pl.BlockSpec((tm, CW), lambda i: (i, 0)), full(qn_g), full(kvn_g), full(wq), full(wkv),
                  full(qg), full(kg), tab, tab, tab],
        out_specs=[pl.BlockSpec((tm, HW), lambda i: (i, 0))] * 3,
        out_shape=[jax.ShapeDtypeStruct((T, HW), BF16)] * 3,
        compiler_params=_cparams(("parallel",), 40), name="mla_prep",
    )(c, qn_g, kvn_g, wq, wkv, qg, kg, *tables)


ATTN_PARTS = 4


def _attn_kernel(q_ref, k_ref, v_ref, o_ref, m_ref, l_ref, acc_ref):
    ki = pl.program_id(3)

    @pl.when(ki == 0)
    def _():
        m_ref[...] = jnp.full_like(m_ref, -jnp.inf)
        l_ref[...] = jnp.zeros_like(l_ref)
        acc_ref[...] = jnp.zeros_like(acc_ref)

    k, v = k_ref[...], v_ref[...]
    part = q_ref.shape[0] // ATTN_PARTS
    rows = [slice(i * part, (i + 1) * part) for i in range(ATTN_PARTS)]
    scores = lambda r: lax.dot_general(q_ref[r, :], k, NT_DIMS, preferred_element_type=F32)
    s_next = scores(rows[0])
    for i, r in enumerate(rows):
        s = s_next
        if i + 1 < ATTN_PARTS:
            s_next = scores(rows[i + 1])
        m_prev = m_ref[r, :]
        m_new = jnp.maximum(m_prev, jnp.max(s, axis=-1, keepdims=True))
        alpha = jnp.exp2(m_prev - m_new)
        p = jnp.exp2(s - m_new)
        l_ref[r, :] = alpha * l_ref[r, :] + jnp.sum(p, axis=-1, keepdims=True)
        acc_ref[r, :] = alpha * acc_ref[r, :] + jnp.dot(p.astype(BF16), v, preferred_element_type=F32)
        m_ref[r, :] = m_new

    @pl.when(ki == pl.num_programs(3) - 1)
    def _():
        o_ref[...] = (acc_ref[...] / l_ref[...]).astype(o_ref.dtype)


def attention(q, k, v, tq, tk):
    B, S, _ = q.shape
    return pl.pallas_call(
        _attn_kernel, grid=(B, MLA_HEADS, S // tq, S // tk),
        in_specs=[pl.BlockSpec((None, tq, MLA_QK_PAD), lambda b, h, i, j: (b, i, h)),
                  pl.BlockSpec((None, tk, MLA_QK_PAD), lambda b, h, i, j: (b, j, h)),
                  pl.BlockSpec((None, tk, MLA_V), lambda b, h, i, j: (b, j, 2 * h))],
        out_specs=pl.BlockSpec((None, tq, MLA_V), lambda b, h, i, j: (b, i, h)),
        out_shape=jax.ShapeDtypeStruct((B, S, MLA_HEADS * MLA_V), BF16),
        scratch_shapes=[pltpu.VMEM((tq, 1), F32), pltpu.VMEM((tq, 1), F32), pltpu.VMEM((tq, MLA_V), F32)],
        compiler_params=_cparams(("parallel", "parallel", "parallel", "arbitrary"), 48), name="mla_attention",
    )(q, k, v)


SUBLANES = 8


def _scan_rows(a, u, h_in, reverse):
    n_group = a.shape[0] // SUBLANES
    sub = lax.broadcasted_iota(I32, (SUBLANES, a.shape[1]), 0)
    scanned = []
    for g in range(n_group):
        rows = slice(g * SUBLANES, (g + 1) * SUBLANES)
        ag, ug = a[rows], u[rows]
        k = 1
        while k < SUBLANES:
            shift, valid = (SUBLANES - k, sub < SUBLANES - k) if reverse else (k, sub >= k)
            a_s, u_s = pltpu.roll(ag, shift, 0), pltpu.roll(ug, shift, 0)
            ug = ug + ag * jnp.where(valid, u_s, 0.0)
            ag = ag * jnp.where(valid, a_s, 1.0)
            k *= 2
        scanned.append((ag, ug))
    outs = [None] * n_group
    h = h_in
    for g in (range(n_group - 1, -1, -1) if reverse else range(n_group)):
        ag, ug = scanned[g]
        outs[g] = ug + ag * h
        h = outs[g][0:1] if reverse else outs[g][SUBLANES - 1:SUBLANES]
    return jnp.concatenate(outs, axis=0), h


def _lru_kernel(*refs, reverse, final):
    if final:
        (x_ref, prev_ref, next_ref, cw_ref, cb_ref, wa_ref, ba_ref, wx_ref, bx_ref, sp_ref,
         gate_ref, hf_ref, o_ref, carry_ref) = refs
    else:
        (x_ref, prev_ref, next_ref, cw_ref, cb_ref, wa_ref, ba_ref, wx_ref, bx_ref, sp_ref,
         o_ref, carry_ref) = refs
    step, nstep = pl.program_id(1), pl.num_programs(1)
    blk = (nstep - 1 - step) if reverse else step

    @pl.when(step == 0)
    def _():
        carry_ref[...] = jnp.zeros_like(carry_ref)

    x = x_ref[...]
    ts = x.shape[0]
    rowi = lax.broadcasted_iota(I32, x.shape, 0)
    has_prev = jnp.where(blk > 0, 1.0, 0.0)
    has_next = jnp.where(blk < nstep - 1, 1.0, 0.0)
    p6, p7, n0 = prev_ref[6:7, :] * has_prev, prev_ref[7:8, :] * has_prev, next_ref[0:1, :] * has_next
    xm1 = jnp.where(rowi == 0, p7, pltpu.roll(x, 1, 0))
    xm2 = jnp.where(rowi == 0, p6, jnp.where(rowi == 1, p7, pltpu.roll(x, 2, 0)))
    xp1 = jnp.where(rowi == ts - 1, n0, pltpu.roll(x, ts - 1, 0))
    cw = cw_ref[...]
    xc = xm2 * cw[0:1, :] + xm1 * cw[1:2, :] + x * cw[2:3, :] + xp1 * cw[3:4, :] + cb_ref[...]
    xcb = xc.astype(BF16)

    def block_diag(w_ref, b_ref):
        outs = [jnp.dot(xcb[:, n * LRU_BW:(n + 1) * LRU_BW], w_ref[n], preferred_element_type=F32)
                for n in range(LRU_BLOCKS)]
        return jnp.concatenate(outs, axis=1) + b_ref[...]

    r = _sigmoid(block_diag(wa_ref, ba_ref))
    i = _sigmoid(block_diag(wx_ref, bx_ref))
    log_a = (-LRU_C) * r * sp_ref[...]
    a = jnp.exp(log_a)
    u = jnp.sqrt(1.0 - a * a) * (i * xc)
    h, carry_ref[...] = _scan_rows(a, u, carry_ref[...], reverse)
    if final:
        g = gate_ref[...]
        gelu = 0.5 * g * (1.0 + jnp.tanh(0.7978845608028654 * (g + 0.044715 * (g * g * g))))
        o_ref[...] = ((h + hf_ref[...]) * gelu).astype(o_ref.dtype)
    else:
        o_ref[...] = h


def lru_direction(proj, cw, cb, wa, ba, wx, bx, sp, reverse, ts, h_fwd=None):
    B, S, W2 = proj.shape
    W = W2 // 2
    nblk = S // ts
    final = h_fwd is not None
    sub = ts // 8

    def pos(c):
        return (nblk - 1 - c) if reverse else c

    tile = (None, ts, W)
    halo = (None, 8, W)
    vec = lambda r: pl.BlockSpec((r, W), lambda b, c: (0, 0))
    wspec = pl.BlockSpec((LRU_BLOCKS, LRU_BW, LRU_BW), lambda b, c: (0, 0, 0))
    in_specs = [pl.BlockSpec(tile, lambda b, c: (b, pos(c), 1)),
                pl.BlockSpec(halo, lambda b, c: (b, jnp.maximum(pos(c) * sub - 1, 0), 1)),
                pl.BlockSpec(halo, lambda b, c: (b, jnp.minimum((pos(c) + 1) * sub, S // 8 - 1), 1)),
                vec(4), vec(1), wspec, vec(1), wspec, vec(1), vec(1)]
    args = [proj, proj, proj, cw, cb, wa, ba, wx, bx, sp]
    if final:
        in_specs += [pl.BlockSpec(tile, lambda b, c: (b, pos(c), 0)), pl.BlockSpec(tile, lambda b, c: (b, pos(c), 0))]
        args += [proj, h_fwd]
    return pl.pallas_call(
        functools.partial(_lru_kernel, reverse=reverse, final=final),
        grid=(B, nblk), in_specs=in_specs, out_specs=pl.BlockSpec(tile, lambda b, c: (b, pos(c), 0)),
        out_shape=jax.ShapeDtypeStruct((B, S, W), BF16 if final else F32),
        scratch_shapes=[pltpu.VMEM((1, W), F32)],
        compiler_params=_cparams(("parallel", "arbitrary"), 48),
        name="lru_bwd" if reverse else "lru_fwd",
    )(*args)


def _router_kernel(x_ref, g_ref, wr_ref, aff_ref, ext_ref):
    x = x_ref[...]
    d = x.shape[1]
    xn = x * _rms_scale(x, d) * g_ref[...]
    wr = wr_ref[...]
    x_hi, w_hi = xn.astype(BF16), wr.astype(BF16)
    x_lo, w_lo = (xn - x_hi.astype(F32)).astype(BF16), (wr - w_hi.astype(F32)).astype(BF16)
    logits = lax.dot_general(jnp.concatenate([w_hi, w_hi, w_lo], axis=1), jnp.concatenate([x_hi, x_lo, x_hi], axis=1),
                             NT_DIMS, preferred_element_type=F32)
    e = jnp.exp(logits - jnp.max(logits, axis=0, keepdims=True))
    aff = e / jnp.sum(e, axis=0, keepdims=True)
    aff_ref[...] = aff
    half = d // 2
    tm = x.shape[0]
    as_bits = lambda v: pltpu.bitcast(v.astype(BF16).astype(F32), I32)
    words = lax.shift_right_logical(as_bits(xn[:, :half]), jnp.int32(16)) | as_bits(xn[:, half:])
    aff_rows = jnp.concatenate([aff, jnp.zeros((LANES - aff.shape[0], aff.shape[1]), F32)], axis=0).T
    n_word_rows = half // LANES
    for s in range(SUBLANES):
        if s < n_word_rows:
            piece = words[:, s * LANES:(s + 1) * LANES]
        elif s == n_word_rows:
            piece = pltpu.bitcast(aff_rows, I32)
        else:
            piece = jnp.zeros((tm, LANES), I32)
        ext_ref[pl.ds(s, tm, stride=SUBLANES), :] = piece


def router(x, g, wr_t, tm):
    N, D = x.shape
    E = wr_t.shape[0]
    return pl.pallas_call(
        _router_kernel, grid=(N // tm,),
        in_specs=[pl.BlockSpec((tm, D), lambda i: (i, 0)), pl.BlockSpec((1, D), lambda i: (0, 0)),
                  pl.BlockSpec((E, D), lambda i: (0, 0))],
        out_specs=[pl.BlockSpec((E, tm), lambda i: (0, i)), pl.BlockSpec((tm * SUBLANES, LANES), lambda i: (i, 0))],
        out_shape=[jax.ShapeDtypeStruct((E, N), F32), jax.ShapeDtypeStruct((N * SUBLANES, LANES), I32)],
        compiler_params=_cparams(("parallel",), 32), name="moe_router",
    )(x, g.reshape(1, D), wr_t)


def _topk_kernel(aff_ref, pos_ref, off_ref, cnt_ref, *, cap):
    bits = pltpu.bitcast(aff_ref[...], I32)
    nb = bits.shape[0]

    def count(pred):
        return jnp.sum(jnp.sum(jnp.where(pred, 1.0, 0.0), axis=1, keepdims=True), axis=0, keepdims=True)

    def bisect(i, lo):
        cand = lo | lax.shift_left(jnp.int32(1), jnp.int32(30) - i)
        return jnp.where(count(bits >= cand) >= cap, cand, lo)

    thr = lax.fori_loop(0, 31, bisect, jnp.zeros((1, 1), I32))

    r128 = lax.broadcasted_iota(I32, (LANES, LANES), 0)
    c128 = lax.broadcasted_iota(I32, (LANES, LANES), 1)
    upper = jnp.where(r128 <= c128, 1.0, 0.0).astype(BF16)
    rn = lax.broadcasted_iota(I32, (nb, nb), 0)
    cn = lax.broadcasted_iota(I32, (nb, nb), 1)
    strict_lower = jnp.where(cn < rn, 1.0, 0.0).astype(BF16)

    def running_count(m):
        incl = jnp.dot(m.astype(BF16), upper, preferred_element_type=F32)
        tot = jnp.broadcast_to(incl[:, LANES - 1:LANES], (nb, LANES)).astype(BF16)
        return incl, jnp.dot(strict_lower, tot, preferred_element_type=F32)

    gt = jnp.where(bits > thr, 1.0, 0.0)
    eq = jnp.where(bits == thr, 1.0, 0.0)
    need = cap - count(bits > thr)
    eq_incl, eq_off = running_count(eq)
    sel = gt + eq * jnp.where(eq_incl - eq + eq_off < need, 1.0, 0.0)
    incl, off = running_count(sel)
    cnt = incl + off
    pos_ref[...] = jnp.where(sel > 0.0, cnt - 1.0, -1.0).astype(I32)
    off_ref[...] = off.astype(I32)
    cnt_ref[...] = cnt


def topk_slots(aff3, cap):
    E, NB, _ = aff3.shape
    spec = pl.BlockSpec((None, NB, LANES), lambda e: (e, 0, 0))
    return pl.pallas_call(
        functools.partial(_topk_kernel, cap=cap), grid=(E,), in_specs=[spec], out_specs=[spec, spec, spec],
        out_shape=[jax.ShapeDtypeStruct((E, NB, LANES), I32), jax.ShapeDtypeStruct((E, NB, LANES), I32),
                   jax.ShapeDtypeStruct((E, NB, LANES), F32)],
        compiler_params=_cparams(("parallel",), 32), name="moe_topk",
    )(aff3)


def _compact_kernel(blo_ref, bhi_ref, cnt_ref, idx_ref, *, n_chunk):
    e = pl.program_id(0)
    slot = lax.broadcasted_iota(I32, (LANES, LANES), 0)

    def chunk(c, carry):
        lo, hi = blo_ref[e, c], bhi_ref[e, c]
        sigma = (slot + c * LANES).astype(F32)

        def blk(b, acc):
            return acc + jnp.where(cnt_ref[b] <= sigma, 1.0, 0.0)

        acc = lax.fori_loop(lo, hi + 1, blk, jnp.zeros((LANES, LANES), F32))
        col = jnp.sum(acc, axis=1, keepdims=True) + (lo * LANES).astype(F32)
        idx_ref[c] = jnp.broadcast_to(col, (LANES, LANES)).T[0:1, :].astype(I32)
        return carry

    lax.fori_loop(0, n_chunk, chunk, 0)


def compact_slots(cnt3, cap):
    E, NB, _ = cnt3.shape
    n_chunk = cap // LANES
    starts = jnp.arange(n_chunk, dtype=F32) * LANES
    blo = jnp.sum(cnt3[:, :, LANES - 1, None] <= starts[None, None, :], axis=1).astype(I32)
    bhi = jnp.sum(cnt3[:, :, 0, None] <= starts[None, None, :] + (LANES - 1), axis=1).astype(I32) - 1
    grid_spec = pltpu.PrefetchScalarGridSpec(
        num_scalar_prefetch=2, grid=(E,),
        in_specs=[pl.BlockSpec((None, NB, 1, LANES), lambda e, *_: (e, 0, 0, 0))],
        out_specs=pl.BlockSpec((None, n_chunk, 1, LANES), lambda e, *_: (e, 0, 0, 0)))
    idx = pl.pallas_call(
        functools.partial(_compact_kernel, n_chunk=n_chunk), grid_spec=grid_spec,
        out_shape=jax.ShapeDtypeStruct((E, n_chunk, 1, LANES), I32),
        compiler_params=_cparams(("arbitrary",), 32), name="moe_compact",
    )(blo, bhi, cnt3.reshape(E, NB, 1, LANES))
    return idx.reshape(E, cap)


def _ffn_kernel(idx0_ref, idx1_ref, idx2_ref, ext_hbm, wg_ref, wu_ref, wd_ref, o_ref, rows_a, rows_b, sem,
                *, steps_per_expert):
    t, n_steps = pl.program_id(0), pl.num_programs(0)
    tm, d = rows_a.shape[0] // SUBLANES, o_ref.shape[1]

    def row_copy(ids_ref, rows_ref, s, r):
        src = ext_hbm.at[pl.ds(pl.multiple_of(ids_ref[0, 0, r], SUBLANES), SUBLANES)]
        return pltpu.make_async_copy(src, rows_ref.at[pl.ds(r * SUBLANES, SUBLANES)], sem.at[s])

    def start_rows(ids_ref, rows_ref, s):
        for r in range(tm):
            row_copy(ids_ref, rows_ref, s, r).start(priority=r % 2)

    def wait_rows(rows_ref, s):
        pltpu.make_async_copy(ext_hbm.at[pl.ds(0, tm * SUBLANES)], rows_ref, sem.at[s]).wait()

    def ffn(rows_ref, out_rows):
        plane = lambda s: rows_ref[pl.ds(s, tm, stride=SUBLANES), :]
        n_word_rows = d // 2 // LANES
        words = jnp.concatenate([plane(s) for s in range(n_word_rows)], axis=1)
        xn = jnp.concatenate([pltpu.bitcast(lax.shift_left(words, jnp.int32(16)), F32),
                              pltpu.bitcast(words & jnp.int32(-65536), F32)], axis=1).astype(BF16)
        lane = lax.broadcasted_iota(I32, (tm, LANES), 1)
        route_gate = jnp.sum(jnp.where(lane == t // steps_per_expert, pltpu.bitcast(plane(n_word_rows), F32), 0.0),
                             axis=1, keepdims=True)
        gate = jnp.dot(xn, wg_ref[...], preferred_element_type=F32)
        up = jnp.dot(xn, wu_ref[...], preferred_element_type=F32)
        hid = (gate * _sigmoid(gate) * up).astype(BF16)
        o_ref[out_rows, :] = (jnp.dot(hid, wd_ref[...], preferred_element_type=F32) * route_gate).astype(o_ref.dtype)

    @pl.when(t == 0)
    def _():
        def first(r, carry):
            row_copy(idx0_ref, rows_a, 0, r).start()
            return carry
        lax.fori_loop(0, tm, first, 0)

    wait_rows(rows_a, 0)
    start_rows(idx1_ref, rows_b, 1)
    ffn(rows_a, slice(0, tm))
    wait_rows(rows_b, 1)
    start_rows(idx2_ref, rows_a, 0)
    ffn(rows_b, slice(tm, 2 * tm))

    @pl.when(t == n_steps - 1)
    def _():
        wait_rows(rows_a, 0)


def expert_ffn(idx, ext, wg, wu, wd, tm):
    E, cap = idx.shape
    D, FF = wg.shape[1], wg.shape[2]
    steps_per_expert = cap // (2 * tm)
    n_tiles = E * cap // tm
    idx3 = (idx * SUBLANES).reshape(n_tiles, 1, tm)
    smem = lambda imap: pl.BlockSpec((1, 1, tm), imap, memory_space=pltpu.SMEM)
    wspec = lambda shape: pl.BlockSpec((None,) + shape, lambda t: (t // steps_per_expert, 0, 0))
    ye = pl.pallas_call(
        functools.partial(_ffn_kernel, steps_per_expert=steps_per_expert), grid=(n_tiles // 2,),
        in_specs=[smem(lambda t: (2 * t, 0, 0)), smem(lambda t: (2 * t + 1, 0, 0)),
                  smem(lambda t: (jnp.minimum(2 * t + 2, n_tiles - 1), 0, 0)),
                  pl.BlockSpec(memory_space=pl.ANY),
                  wspec((D, FF)), wspec((D, FF)), wspec((FF, D))],
        out_specs=pl.BlockSpec((2 * tm, D), lambda t: (t, 0)),
        out_shape=jax.ShapeDtypeStruct((E * cap, D), BF16),
        scratch_shapes=[pltpu.VMEM((tm * SUBLANES, LANES), I32), pltpu.VMEM((tm * SUBLANES, LANES), I32),
                        pltpu.SemaphoreType.DMA((2,))],
        compiler_params=_cparams(("arbitrary",), 56), name="moe_ffn",
    )(idx3, idx3, idx3, ext, wg, wu, wd)
    return ye.reshape(E, cap, D)


COMBINE_TOKENS = 128


COMBINE_ALIGN = 16
COMBINE_SMALL = 48


def _combine_kernel(s0_small_ref, s0_full_ref, needs_full_ref, x_ref, pos_ref, ye_hbm, o_ref,
                    small_ref, full_ref, sem_small, sem_full):
    t, n_tiles = pl.program_id(0), pl.num_programs(0)
    n_exp = pos_ref.shape[1]
    small, full = small_ref.shape[1] // n_exp, full_ref.shape[1]
    par = t % 2

    def small_window(tile, e, p):
        s0 = pl.multiple_of(s0_small_ref[e, tile], COMBINE_ALIGN)
        return pltpu.make_async_copy(ye_hbm.at[e, pl.ds(s0, small)], small_ref.at[p, pl.ds(e * small, small)],
                                     sem_small.at[p, e])

    def full_window(e):
        s0 = pl.multiple_of(s0_full_ref[e, t], COMBINE_ALIGN)
        return pltpu.make_async_copy(ye_hbm.at[e, pl.ds(s0, full)], full_ref.at[e], sem_full.at[e])

    def scatter_sum(s0_ref, win, rows_of):
        lane = lax.broadcasted_iota(I32, (x_ref.shape[0], win), 1)
        onehots = [jnp.where(pos_ref[:, e:e + 1] == lane + s0_ref[e, t], 1.0, 0.0).astype(BF16)
                   for e in range(n_exp)]
        parts = [jnp.dot(onehots[e], rows_of(e), preferred_element_type=F32) for e in range(n_exp)]
        while len(parts) > 1:
            parts = [a + b for a, b in zip(parts[0::2], parts[1::2])]
        return x_ref[...] + parts[0]

    @pl.when(t == 0)
    def _():
        for e in range(n_exp):
            small_window(0, e, 0).start()

    @pl.when(t + 1 < n_tiles)
    def _():
        for e in range(n_exp):
            small_window(t + 1, e, 1 - par).start(priority=e % 2)

    for e in range(n_exp):
        small_window(t, e, par).wait()

    @pl.when(needs_full_ref[t] == 0)
    def _():
        width = n_exp * small
        col = lax.broadcasted_iota(I32, (1, width), 1)
        owner = col // small
        target = col - owner * small
        wanted = jnp.full((x_ref.shape[0], width), -2, I32)
        for e in range(n_exp):
            target = target + jnp.where(owner == e, s0_small_ref[e, t], 0)
            wanted = jnp.where(owner == e, pos_ref[:, e:e + 1], wanted)
        onehot = jnp.where(wanted == target, 1.0, 0.0).astype(BF16)
        o_ref[...] = x_ref[...] + jnp.dot(onehot, small_ref[par], preferred_element_type=F32)

    @pl.when(needs_full_ref[t] != 0)
    def _():
        for e in range(n_exp):
            full_window(e).start()
        for e in range(n_exp):
            full_window(e).wait()
        o_ref[...] = scatter_sum(s0_full_ref, full, lambda e: full_ref[e])


def moe_combine(x, pos_t, off, ye, tt):
    N, D = x.shape
    E, cap, _ = ye.shape
    full = tt + COMBINE_ALIGN
    small = min(COMBINE_SMALL, full)
    aligned = (off // COMBINE_ALIGN) * COMBINE_ALIGN
    s0_small = jnp.clip(aligned, 0, cap - small).astype(I32)
    s0_full = jnp.clip(aligned, 0, cap - full).astype(I32)
    end = jnp.concatenate([off[:, 1:], jnp.full((E, 1), cap, off.dtype)], axis=1)
    needs_full = jnp.any(end > s0_small + small, axis=0).astype(I32)
    grid_spec = pltpu.PrefetchScalarGridSpec(
        num_scalar_prefetch=3, grid=(N // tt,),
        in_specs=[pl.BlockSpec((tt, D), lambda t, *_: (t, 0)), pl.BlockSpec((tt, E), lambda t, *_: (t, 0)),
                  pl.BlockSpec(memory_space=pl.ANY)],
        out_specs=pl.BlockSpec((tt, D), lambda t, *_: (t, 0)),
        scratch_shapes=[pltpu.VMEM((2, E * small, D), BF16), pltpu.VMEM((E, full, D), BF16),
                        pltpu.SemaphoreType.DMA((2, E)), pltpu.SemaphoreType.DMA((E,))])
    return pl.pallas_call(
        _combine_kernel, grid_spec=grid_spec,
        out_shape=jax.ShapeDtypeStruct((N, D), F32),
        compiler_params=_cparams(("arbitrary",), 40), name="moe_combine",
    )(s0_small, s0_full, needs_full, x, pos_t, ye)


def expert_choice_moe(x, g, wr_t, wg, wu, wd):
    N, D = x.shape
    E = wr_t.shape[0]
    cap = CAPACITY * N // E
    nb = N // LANES
    aff, ext = router(x, g, wr_t, 512)
    pos3, off3, cnt3 = topk_slots(aff.reshape(E, nb, LANES), cap)
    idx = compact_slots(cnt3, cap)
    ye = expert_ffn(idx, ext, wg, wu, wd, min(512, cap // 2))
    return moe_combine(x, pos3.reshape(E, N).T, off3[:, ::COMBINE_TOKENS // LANES, 0], ye, COMBINE_TOKENS)


def _rope_tables(seq):
    pos = jnp.arange(seq, dtype=F32)
    inv = 1.0 / (ROPE_THETA ** (jnp.arange(0, MLA_ROPE, 2, dtype=F32) / MLA_ROPE))
    ang = pos[:, None] * inv[None, :]
    cos, sin = jnp.cos(ang), jnp.sin(ang)
    z32, z64 = jnp.zeros_like(cos), jnp.zeros((seq, LANES - MLA_ROPE), F32)
    return (jnp.concatenate([cos, cos, z64], axis=1), jnp.concatenate([-sin, z32, z64], axis=1),
            jnp.concatenate([z32, sin, z64], axis=1))


def _pad_cols(a, width):
    return jnp.pad(a, [(0, 0)] * (a.ndim - 1) + [(0, width - a.shape[-1])])


def even_layer(x, batch, seq, g_mix, w_in, w_out, lb, hg_gain, qn_g, kvn_g, w_uq, w_ukv, q_gain, k_gain):
    T, D = x.shape
    hgw = 5 * HG_WIDTH
    w_hg = w_in[:, :hgw].astype(BF16)
    w_mla = _pad_cols(w_in[:, hgw:], 6 * LANES).astype(BF16)
    proj_hg, proj_mla = norm_matmul(x, g_mix, [w_hg, w_mla], [F32, F32], 256)

    proj3 = proj_hg.reshape(batch, seq, hgw)
    rows = 256
    o_f = hgrn_direction(proj3, lb[0:1], False, rows)
    o_hg = hgrn_direction(proj3, lb[1:2], True, rows, o_fwd=o_f, gain=hg_gain)

    wq = _pad_cols(w_uq.reshape(MLA_Q_RANK, MLA_HEADS, MLA_QK), MLA_QK_PAD).reshape(MLA_Q_RANK, -1).astype(BF16)
    wkv = w_ukv.astype(BF16)
    qg = _pad_cols(q_gain.reshape(1, MLA_QK), MLA_QK_PAD)
    kg = _pad_cols(k_gain.reshape(1, MLA_QK), MLA_QK_PAD)
    q, k, v = mla_prep(proj_mla, qn_g.reshape(1, -1), kvn_g.reshape(1, -1), wq, wkv, qg, kg,
                       _rope_tables(seq), seq, 512)
    hq = MLA_HEADS * MLA_QK_PAD
    tq, tk = min(seq, 1024), min(seq, 2048)
    o_mla = attention(q.reshape(batch, seq, hq), k.reshape(batch, seq, hq),
                      v.reshape(batch, seq, hq), tq, tk)

    w_o = w_out.astype(BF16)
    return matmul_resid(x, [o_hg.reshape(T, HG_WIDTH), o_mla.reshape(T, MLA_HEADS * MLA_V)],
                        [w_o[:HG_WIDTH], w_o[HG_WIDTH:]], 512)


def odd_layer(x, batch, seq, g_mix, w_in, conv_w, conv_b, wa, ba, wx, bx, lam, w_out):
    T, D = x.shape
    (proj,) = norm_matmul(x, g_mix, [w_in.astype(BF16)], [F32], 256)
    proj3 = proj.reshape(batch, seq, 2 * D)
    sp = jax.nn.softplus(-lam.astype(F32))
    cb = conv_b.reshape(1, D)
    ts = 128
    h_f = lru_direction(proj3, conv_w, cb, wa[0].astype(BF16), ba[0:1], wx[0].astype(BF16), bx[0:1], sp[0:1],
                        False, ts)
    y = lru_direction(proj3, conv_w, cb, wa[1].astype(BF16), ba[1:2], wx[1].astype(BF16), bx[1:2], sp[1:2],
                      True, ts, h_fwd=h_f)
    return matmul_resid(x, [y.reshape(T, D)], [w_out.astype(BF16)], 512)


def _trunk(x3, norm_mix, norm_ffn, ev_w_in, ev_w_out, hg_lb_logits, hg_out_gain, mla_q_norm, mla_kv_norm,
           mla_w_uq, mla_w_ukv, mla_q_gain, mla_k_gain, od_w_in, od_conv_w, od_conv_b, rg_w_a, rg_b_a,
           rg_w_x, rg_b_x, rg_lambda, od_w_out, moe_router, moe_w_gate, moe_w_up, moe_w_down):
    batch, seq, D = x3.shape
    x = x3.reshape(batch * seq, D)
    lb_all = jnp.cumsum(jax.nn.softmax(hg_lb_logits.astype(F32), axis=1), axis=1)
    depth = norm_mix.shape[0]
    for layer in range(depth):
        j = layer // 2
        if layer % 2 == 0:
            x = even_layer(x, batch, seq, norm_mix[layer], ev_w_in[j], ev_w_out[j], lb_all[:, j],
                           hg_out_gain[j], mla_q_norm[j], mla_kv_norm[j], mla_w_uq[j], mla_w_ukv[j],
                           mla_q_gain[j], mla_k_gain[j])
        else:
            x = odd_layer(x, batch, seq, norm_mix[layer], od_w_in[j], od_conv_w[j], od_conv_b[j], rg_w_a[j],
                          rg_b_a[j], rg_w_x[j], rg_b_x[j], rg_lambda[j], od_w_out[j])
        x = expert_choice_moe(x, norm_ffn[layer], moe_router[layer].T, moe_w_gate[layer].astype(BF16),
                              moe_w_up[layer].astype(BF16), moe_w_down[layer].astype(BF16))
    return x.reshape(batch, seq, D)


def kernel(x_prompt, x_sample, norm_mix, norm_ffn, ev_w_in, ev_w_out, hg_lb_logits, hg_out_gain, mla_q_norm,
           mla_kv_norm, mla_w_uq, mla_w_ukv, mla_q_gain, mla_k_gain, od_w_in, od_conv_w, od_conv_b, rg_w_a,
           rg_b_a, rg_w_x, rg_b_x, rg_lambda, od_w_out, moe_router, moe_w_gate, moe_w_up, moe_w_down):
    params = (norm_mix, norm_ffn, ev_w_in, ev_w_out, hg_lb_logits, hg_out_gain, mla_q_norm, mla_kv_norm,
              mla_w_uq, mla_w_ukv, mla_q_gain, mla_k_gain, od_w_in, od_conv_w, od_conv_b, rg_w_a, rg_b_a,
              rg_w_x, rg_b_x, rg_lambda, od_w_out, moe_router, moe_w_gate, moe_w_up, moe_w_down)
    return (_trunk(x_prompt, *params), _trunk(x_sample, *params))
```

```python
import functools

import jax
import jax.numpy as jnp
from jax import lax
from jax.experimental import pallas as pl
from jax.experimental.pallas import tpu as pltpu

F32 = jnp.float32
BF16 = jnp.bfloat16
I32 = jnp.int32
EPS = 1e-6
HIGHEST = lax.Precision.HIGHEST
NT_DIMS = (((1,), (1,)), ((), ()))
TN_DIMS = (((0,), (0,)), ((), ()))

D_MODEL = 1024
HG_HEADS = 4
HG_DK = 128
HG_WIDTH = 512
HG_CHUNK = 64
HG_SUB = 16
HG_EXP_CLAMP = 80.0
MLA_HEADS = 4
MLA_Q_RANK = 384
MLA_KV_RANK = 256
MLA_NOPE = 128
MLA_ROPE = 64
MLA_V = 128
MLA_QK = MLA_NOPE + MLA_ROPE
MLA_QK_PAD = 256
ROPE_THETA = 10000.0
LRU_BLOCKS = 4
LRU_BW = 256
LRU_C = 8.0
N_EXPERTS = 16
CAPACITY = 2
LANES = 128
LOG2_E = 1.4426950408889634
MIB = 1024 * 1024


def _cparams(semantics, vmem_mib):
    return pltpu.CompilerParams(dimension_semantics=semantics, vmem_limit_bytes=vmem_mib * MIB)


def _sigmoid(x):
    return 1.0 / (1.0 + jnp.exp(-x))


def _rms_scale(x, width):
    return lax.rsqrt(jnp.sum(x * x, axis=-1, keepdims=True) * (1.0 / width) + EPS)


def _norm_matmul_kernel(x_ref, g_ref, *refs, n_out):
    x = x_ref[...]
    h = (x * _rms_scale(x, x.shape[-1]) * g_ref[...]).astype(BF16)
    for w_ref, o_ref in zip(refs[:n_out], refs[n_out:]):
        o_ref[...] = jnp.dot(h, w_ref[...], preferred_element_type=F32).astype(o_ref.dtype)


def norm_matmul(x, g, ws, out_dtypes, tm):
    T, D = x.shape
    n = len(ws)
    in_specs = [pl.BlockSpec((tm, D), lambda i: (i, 0)), pl.BlockSpec((1, D), lambda i: (0, 0))]
    in_specs += [pl.BlockSpec(w.shape, lambda i: (0, 0)) for w in ws]
    out_specs = [pl.BlockSpec((tm, w.shape[1]), lambda i: (i, 0)) for w in ws]
    out_shape = [jax.ShapeDtypeStruct((T, w.shape[1]), dt) for w, dt in zip(ws, out_dtypes)]
    return pl.pallas_call(
        functools.partial(_norm_matmul_kernel, n_out=n),
        grid=(T // tm,), in_specs=in_specs, out_specs=out_specs, out_shape=out_shape,
        compiler_params=_cparams(("parallel",), 56), name="norm_matmul",
    )(x, g.reshape(1, D), *ws)


def _matmul_resid_kernel(r_ref, *refs, n_in):
    acc = r_ref[...]
    for a_ref, w_ref in zip(refs[:n_in], refs[n_in:2 * n_in]):
        acc = acc + jnp.dot(a_ref[...], w_ref[...], preferred_element_type=F32)
    refs[2 * n_in][...] = acc


def matmul_resid(resid, acts, ws, tm):
    T, D = resid.shape
    n = len(acts)
    in_specs = [pl.BlockSpec((tm, D), lambda i: (i, 0))]
    in_specs += [pl.BlockSpec((tm, a.shape[1]), lambda i: (i, 0)) for a in acts]
    in_specs += [pl.BlockSpec(w.shape, lambda i: (0, 0)) for w in ws]
    return pl.pallas_call(
        functools.partial(_matmul_resid_kernel, n_in=n),
        grid=(T // tm,), in_specs=in_specs, out_specs=pl.BlockSpec((tm, D), lambda i: (i, 0)),
        out_shape=jax.ShapeDtypeStruct((T, D), F32),
        compiler_params=_cparams(("parallel",), 40), name="matmul_resid",
    )(resid, *acts, *ws)


def _hgrn_kernel(*refs, reverse, final, n_chunk):
    if final:
        q_ref, v_ref, fz_ref, lb_ref, g_ref, of_ref, gain_ref, o_ref, st_ref = refs
    else:
        q_ref, v_ref, fz_ref, lb_ref, o_ref, st_ref = refs
    C, SB = HG_CHUNK, HG_SUB

    @pl.when(pl.program_id(1) == 0)
    def _():
        st_ref[...] = jnp.zeros_like(st_ref)

    row = lax.broadcasted_iota(I32, (C, C), 0)
    col = lax.broadcasted_iota(I32, (C, C), 1)
    blk_of_row = row // SB
    if reverse:
        tri = col >= row
        ref_of_row = blk_of_row * SB
        causal = col >= row
    else:
        tri = col <= row
        ref_of_row = blk_of_row * SB + (SB - 1)
        causal = col <= row
    ref_sum = (col >= ref_of_row) if reverse else (col <= ref_of_row)
    cum_mat = jnp.concatenate([jnp.where(tri, 1.0, 0.0), jnp.where(ref_sum, 1.0, 0.0)], axis=0)
    cum_mat3 = jnp.concatenate([cum_mat, cum_mat, cum_mat], axis=1).astype(BF16)
    row_blk = lax.broadcasted_iota(I32, (C, HG_WIDTH), 0) // SB

    heads = [slice(h * HG_DK, (h + 1) * HG_DK) for h in range(HG_HEADS)]
    order = [((n_chunk - 1 - ci) if reverse else ci) for ci in range(n_chunk)]
    lb = lb_ref[...]

    per_chunk = []
    for c in order:
        rows = slice(c * C, (c + 1) * C)
        q = q_ref[rows, :] * (HG_DK ** -0.5)
        f = lb + (1.0 - lb) * _sigmoid(fz_ref[rows, :])
        logf = jnp.log(f)
        k = 1.0 - f
        hi = logf.astype(BF16)
        r1 = logf - hi.astype(F32)
        mid = r1.astype(BF16)
        lo = (r1 - mid.astype(F32)).astype(BF16)
        bb = jnp.dot(cum_mat3, jnp.concatenate([hi, mid, lo], axis=0), preferred_element_type=F32)
        b, b_blk = bb[:C], bb[C:]
        kt = k * jnp.exp(b_blk - b)
        q_parts, k_parts = [], []
        for j in range(C // SB):
            jr = j * SB if reverse else j * SB + SB - 1
            q_parts.append((q * jnp.exp(jnp.minimum(b - b[jr:jr + 1, :], HG_EXP_CLAMP))).astype(BF16))
            k_parts.append(jnp.where(row_blk == j, kt, 0.0).astype(BF16))
        b_end = b[0:1, :] if reverse else b[C - 1:C, :]
        vb = v_ref[rows, :].astype(BF16)
        intra = []
        for cs in heads:
            scores = lax.dot_general(jnp.concatenate([p[:, cs] for p in q_parts], axis=1),
                                     jnp.concatenate([p[:, cs] for p in k_parts], axis=1),
                                     NT_DIMS, preferred_element_type=F32)
            scores = jnp.where(causal, scores, 0.0).astype(BF16)
            intra.append(jnp.dot(scores, vb[:, cs], preferred_element_type=F32))
        per_chunk.append((rows, intra, (q * jnp.exp(b)).astype(BF16), (k * jnp.exp(b_end - b)).astype(BF16),
                          jnp.exp(b_end), vb))

    states = [st_ref[h] for h in range(HG_HEADS)]
    for rows, intra, q_dec, k_hat, decay_end, vb in per_chunk:
        for h, cs in enumerate(heads):
            o = intra[h] + lax.dot_general(q_dec[:, cs], states[h].astype(BF16), NT_DIMS,
                                           preferred_element_type=F32)
            states[h] = states[h] * decay_end[:, cs] + lax.dot_general(
                vb[:, cs], k_hat[:, cs], TN_DIMS, preferred_element_type=F32)
            if final:
                o = o + of_ref[rows, cs]
                y = o * _rms_scale(o, HG_DK) * gain_ref[...]
                g = g_ref[rows, cs]
                o_ref[rows, cs] = (y * (g * _sigmoid(g))).astype(o_ref.dtype)
            else:
                o_ref[rows, cs] = o
    for h in range(HG_HEADS):
        st_ref[h] = states[h]


def hgrn_direction(proj, lb, reverse, rows, o_fwd=None, gain=None):
    B, S, _ = proj.shape
    W = HG_WIDTH
    nblk = S // rows
    final = o_fwd is not None

    def rowmap(colblk):
        if reverse:
            return lambda b, c: (b, nblk - 1 - c, colblk)
        return lambda b, c: (b, c, colblk)

    blk = (None, rows, W)
    in_specs = [pl.BlockSpec(blk, rowmap(0)), pl.BlockSpec(blk, rowmap(1)),
                pl.BlockSpec(blk, rowmap(4 if reverse else 3)), pl.BlockSpec((1, W), lambda b, c: (0, 0))]
    args = [proj, proj, proj, lb]
    if final:
        in_specs += [pl.BlockSpec(blk, rowmap(2)), pl.BlockSpec(blk, rowmap(0)),
                     pl.BlockSpec((1, HG_DK), lambda b, c: (0, 0))]
        args += [proj, o_fwd, gain.reshape(1, HG_DK)]
    return pl.pallas_call(
        functools.partial(_hgrn_kernel, reverse=reverse, final=final, n_chunk=rows // HG_CHUNK),
        grid=(B, nblk), in_specs=in_specs, out_specs=pl.BlockSpec(blk, rowmap(0)),
        out_shape=jax.ShapeDtypeStruct((B, S, W), BF16 if final else F32),
        scratch_shapes=[pltpu.VMEM((HG_HEADS, HG_DK, HG_DK), F32)],
        compiler_params=_cparams(("parallel", "arbitrary"), 40),
        name="hgrn_bwd" if reverse else "hgrn_fwd",
    )(*args)


def _rope(x, cos_t, sin_a, sin_b):
    return x * cos_t + pltpu.roll(x, LANES - MLA_ROPE // 2, 1) * sin_a + pltpu.roll(x, MLA_ROPE // 2, 1) * sin_b


def _mla_prep_kernel(c_ref, qn_ref, kvn_ref, wq_ref, wkv_ref, qg_ref, kg_ref, cos_ref, sa_ref, sb_ref,
                     q_ref, k_ref, v_ref):
    c = c_ref[...]
    cq = c[:, :MLA_Q_RANK]
    ckv = c[:, MLA_Q_RANK:MLA_Q_RANK + MLA_KV_RANK]
    kpe = c[:, MLA_Q_RANK + MLA_KV_RANK:]
    qn = (cq * _rms_scale(cq, MLA_Q_RANK) * qn_ref[...]).astype(BF16)
    kvn = (ckv * _rms_scale(ckv, MLA_KV_RANK) * kvn_ref[...]).astype(BF16)
    qf = jnp.dot(qn, wq_ref[...], preferred_element_type=F32)
    kvf = jnp.dot(kvn, wkv_ref[...], preferred_element_type=F32)
    cos_t, sin_a, sin_b = cos_ref[...], sa_ref[...], sb_ref[...]
    qg, kg = qg_ref[...], kg_ref[...]
    sm_scale = (MLA_QK ** -0.5) * LOG2_E
    kpe_ss = jnp.sum(kpe * kpe, axis=-1, keepdims=True)
    for h in range(MLA_HEADS):
        lo = h * MLA_QK_PAD
        q_nope, q_rope = qf[:, lo:lo + LANES], qf[:, lo + LANES:lo + 2 * LANES]
        qs = lax.rsqrt((jnp.sum(q_nope * q_nope, axis=-1, keepdims=True)
                        + jnp.sum(q_rope * q_rope, axis=-1, keepdims=True)) * (1.0 / MLA_QK) + EPS)
        q_ref[:, lo:lo + LANES] = (q_nope * qs * qg[:, :LANES] * sm_scale).astype(BF16)
        q_ref[:, lo + LANES:lo + 2 * LANES] = (
            _rope(q_rope * qs * qg[:, LANES:], cos_t, sin_a, sin_b) * sm_scale).astype(BF16)
        k_nope = kvf[:, lo:lo + LANES]
        ks = lax.rsqrt((jnp.sum(k_nope * k_nope, axis=-1, keepdims=True) + kpe_ss) * (1.0 / MLA_QK) + EPS)
        k_ref[:, lo:lo + LANES] = (k_nope * ks * kg[:, :LANES]).astype(BF16)
        k_ref[:, lo + LANES:lo + 2 * LANES] = _rope(kpe * ks * kg[:, LANES:], cos_t, sin_a, sin_b).astype(BF16)
        v_ref[:, lo:lo + LANES] = kvf[:, lo + LANES:lo + 2 * LANES].astype(BF16)
        v_ref[:, lo + LANES:lo + 2 * LANES] = jnp.ones((c.shape[0], LANES), BF16)


def mla_prep(c, qn_g, kvn_g, wq, wkv, qg, kg, tables, seq, tm):
    T, CW = c.shape
    nseq = seq // tm
    full = lambda a: pl.BlockSpec(a.shape, lambda i: (0, 0))
    tab = pl.BlockSpec((tm, LANES), lambda i: (i % nseq, 0))
    HW = MLA_HEADS * MLA_QK_PAD
    return pl.pallas_call(
        _mla_prep_kernel, grid=(T // tm,),
        in_specs=[pl.BlockSpec((tm, CW), lambda i: (i, 0)), full(qn_g), full(kvn_g), full(wq), full(wkv),
                  full(qg), full(kg), tab, tab, tab],
        out_specs=[pl.BlockSpec((tm, HW), lambda i: (i, 0))] * 3,
        out_shape=[jax.ShapeDtypeStruct((T, HW), BF16)] * 3,
        compiler_params=_cparams(("parallel",), 40), name="mla_prep",
    )(c, qn_g, kvn_g, wq, wkv, qg, kg, *tables)


ATTN_PARTS = 8


def _attn_kernel(q_ref, k_ref, v_ref, o_ref, m_ref, l_ref, acc_ref):
    ki = pl.program_id(3)

    @pl.when(ki == 0)
    def _():
        m_ref[...] = jnp.full_like(m_ref, -jnp.inf)
        l_ref[...] = jnp.zeros_like(l_ref)
        acc_ref[...] = jnp.zeros_like(acc_ref)

    k, v = k_ref[...], v_ref[...]
    part = q_ref.shape[0] // ATTN_PARTS
    rows = [slice(i * part, (i + 1) * part) for i in range(ATTN_PARTS)]
    scores = lambda r: lax.dot_general(q_ref[r, :], k, NT_DIMS, preferred_element_type=F32)
    s_next = scores(rows[0])
    for i, r in enumerate(rows):
        s = s_next
        if i + 1 < ATTN_PARTS:
            s_next = scores(rows[i + 1])
        m_prev = m_ref[r, :]
        m_new = jnp.maximum(m_prev, jnp.max(s, axis=-1, keepdims=True))
        alpha = jnp.exp2(m_prev - m_new)
        p = jnp.exp2(s - m_new)
        l_ref[r, :] = alpha * l_ref[r, :] + jnp.sum(p, axis=-1, keepdims=True)
        acc_ref[r, :] = alpha * acc_ref[r, :] + jnp.dot(p.astype(BF16), v, preferred_element_type=F32)
        m_ref[r, :] = m_new

    @pl.when(ki == pl.num_programs(3) - 1)
    def _():
        o_ref[...] = (acc_ref[...] / l_ref[...]).astype(o_ref.dtype)


def attention(q, k, v, tq, tk):
    B, S, _ = q.shape
    return pl.pallas_call(
        _attn_kernel, grid=(B, MLA_HEADS, S // tq, S // tk),
        in_specs=[pl.BlockSpec((None, tq, MLA_QK_PAD), lambda b, h, i, j: (b, i, h)),
                  pl.BlockSpec((None, tk, MLA_QK_PAD), lambda b, h, i, j: (b, j, h)),
                  pl.BlockSpec((None, tk, MLA_V), lambda b, h, i, j: (b, j, 2 * h))],
        out_specs=pl.BlockSpec((None, tq, MLA_V), lambda b, h, i, j: (b, i, h)),
        out_shape=jax.ShapeDtypeStruct((B, S, MLA_HEADS * MLA_V), BF16),
        scratch_shapes=[pltpu.VMEM((tq, 1), F32), pltpu.VMEM((tq, 1), F32), pltpu.VMEM((tq, MLA_V), F32)],
        compiler_params=_cparams(("parallel", "parallel", "parallel", "arbitrary"), 48), name="mla_attention",
    )(q, k, v)


SUBLANES = 8


def _scan_rows(a, u, h_in, reverse):
    n_group = a.shape[0] // SUBLANES
    sub = lax.broadcasted_iota(I32, (SUBLANES, a.shape[1]), 0)
    scanned = []
    for g in range(n_group):
        rows = slice(g * SUBLANES, (g + 1) * SUBLANES)
        ag, ug = a[rows], u[rows]
        k = 1
        while k < SUBLANES:
            shift, valid = (SUBLANES - k, sub < SUBLANES - k) if reverse else (k, sub >= k)
            a_s, u_s = pltpu.roll(ag, shift, 0), pltpu.roll(ug, shift, 0)
            ug = ug + ag * jnp.where(valid, u_s, 0.0)
            ag = ag * jnp.where(valid, a_s, 1.0)
            k *= 2
        scanned.append((ag, ug))
    outs = [None] * n_group
    h = h_in
    for g in (range(n_group - 1, -1, -1) if reverse else range(n_group)):
        ag, ug = scanned[g]
        outs[g] = ug + ag * h
        h = outs[g][0:1] if reverse else outs[g][SUBLANES - 1:SUBLANES]
    return jnp.concatenate(outs, axis=0), h


def _lru_kernel(*refs, reverse, final):
    if final:
        (x_ref, prev_ref, next_ref, cw_ref, cb_ref, wa_ref, ba_ref, wx_ref, bx_ref, sp_ref,
         gate_ref, hf_ref, o_ref, carry_ref) = refs
    else:
        (x_ref, prev_ref, next_ref, cw_ref, cb_ref, wa_ref, ba_ref, wx_ref, bx_ref, sp_ref,
         o_ref, carry_ref) = refs
    step, nstep = pl.program_id(1), pl.num_programs(1)
    blk = (nstep - 1 - step) if reverse else step

    @pl.when(step == 0)
    def _():
        carry_ref[...] = jnp.zeros_like(carry_ref)

    x = x_ref[...]
    ts = x.shape[0]
    rowi = lax.broadcasted_iota(I32, x.shape, 0)
    has_prev = jnp.where(blk > 0, 1.0, 0.0)
    has_next = jnp.where(blk < nstep - 1, 1.0, 0.0)
    p6, p7, n0 = prev_ref[6:7, :] * has_prev, prev_ref[7:8, :] * has_prev, next_ref[0:1, :] * has_next
    xm1 = jnp.where(rowi == 0, p7, pltpu.roll(x, 1, 0))
    xm2 = jnp.where(rowi == 0, p6, jnp.where(rowi == 1, p7, pltpu.roll(x, 2, 0)))
    xp1 = jnp.where(rowi == ts - 1, n0, pltpu.roll(x, ts - 1, 0))
    cw = cw_ref[...]
    xc = xm2 * cw[0:1, :] + xm1 * cw[1:2, :] + x * cw[2:3, :] + xp1 * cw[3:4, :] + cb_ref[...]
    xcb = xc.astype(BF16)

    def block_diag(w_ref, b_ref):
        outs = [jnp.dot(xcb[:, n * LRU_BW:(n + 1) * LRU_BW], w_ref[n], preferred_element_type=F32)
                for n in range(LRU_BLOCKS)]
        return jnp.concatenate(outs, axis=1) + b_ref[...]

    r = _sigmoid(block_diag(wa_ref, ba_ref))
    i = _sigmoid(block_diag(wx_ref, bx_ref))
    log_a = (-LRU_C) * r * sp_ref[...]
    a = jnp.exp(log_a)
    u = jnp.sqrt(1.0 - a * a) * (i * xc)
    h, carry_ref[...] = _scan_rows(a, u, carry_ref[...], reverse)
    if final:
        g = gate_ref[...]
        gelu = 0.5 * g * (1.0 + jnp.tanh(0.7978845608028654 * (g + 0.044715 * (g * g * g))))
        o_ref[...] = ((h + hf_ref[...]) * gelu).astype(o_ref.dtype)
    else:
        o_ref[...] = h


def lru_direction(proj, cw, cb, wa, ba, wx, bx, sp, reverse, ts, h_fwd=None):
    B, S, W2 = proj.shape
    W = W2 // 2
    nblk = S // ts
    final = h_fwd is not None
    sub = ts // 8

    def pos(c):
        return (nblk - 1 - c) if reverse else c

    tile = (None, ts, W)
    halo = (None, 8, W)
    vec = lambda r: pl.BlockSpec((r, W), lambda b, c: (0, 0))
    wspec = pl.BlockSpec((LRU_BLOCKS, LRU_BW, LRU_BW), lambda b, c: (0, 0, 0))
    in_specs = [pl.BlockSpec(tile, lambda b, c: (b, pos(c), 1)),
                pl.BlockSpec(halo, lambda b, c: (b, jnp.maximum(pos(c) * sub - 1, 0), 1)),
                pl.BlockSpec(halo, lambda b, c: (b, jnp.minimum((pos(c) + 1) * sub, S // 8 - 1), 1)),
                vec(4), vec(1), wspec, vec(1), wspec, vec(1), vec(1)]
    args = [proj, proj, proj, cw, cb, wa, ba, wx, bx, sp]
    if final:
        in_specs += [pl.BlockSpec(tile, lambda b, c: (b, pos(c), 0)), pl.BlockSpec(tile, lambda b, c: (b, pos(c), 0))]
        args += [proj, h_fwd]
    return pl.pallas_call(
        functools.partial(_lru_kernel, reverse=reverse, final=final),
        grid=(B, nblk), in_specs=in_specs, out_specs=pl.BlockSpec(tile, lambda b, c: (b, pos(c), 0)),
        out_shape=jax.ShapeDtypeStruct((B, S, W), BF16 if final else F32),
        scratch_shapes=[pltpu.VMEM((1, W), F32)],
        compiler_params=_cparams(("parallel", "arbitrary"), 48),
        name="lru_bwd" if reverse else "lru_fwd",
    )(*args)


def _router_kernel(x_ref, g_ref, wr_ref, aff_ref, ext_ref):
    x = x_ref[...]
    d = x.shape[1]
    xn = x * _rms_scale(x, d) * g_ref[...]
    wr = wr_ref[...]
    x_hi, w_hi = xn.astype(BF16), wr.astype(BF16)
    x_lo, w_lo = (xn - x_hi.astype(F32)).astype(BF16), (wr - w_hi.astype(F32)).astype(BF16)
    logits = lax.dot_general(jnp.concatenate([w_hi, w_hi, w_lo], axis=1), jnp.concatenate([x_hi, x_lo, x_hi], axis=1),
                             NT_DIMS, preferred_element_type=F32)
    e = jnp.exp(logits - jnp.max(logits, axis=0, keepdims=True))
    aff = e / jnp.sum(e, axis=0, keepdims=True)
    aff_ref[...] = aff
    half = d // 2
    tm = x.shape[0]
    as_bits = lambda v: pltpu.bitcast(v.astype(BF16).astype(F32), I32)
    words = lax.shift_right_logical(as_bits(xn[:, :half]), jnp.int32(16)) | as_bits(xn[:, half:])
    aff_rows = jnp.concatenate([aff, jnp.zeros((LANES - aff.shape[0], aff.shape[1]), F32)], axis=0).T
    n_word_rows = half // LANES
    for s in range(SUBLANES):
        if s < n_word_rows:
            piece = words[:, s * LANES:(s + 1) * LANES]
        elif s == n_word_rows:
            piece = pltpu.bitcast(aff_rows, I32)
        else:
            piece = jnp.zeros((tm, LANES), I32)
        ext_ref[pl.ds(s, tm, stride=SUBLANES), :] = piece


def router(x, g, wr_t, tm):
    N, D = x.shape
    E = wr_t.shape[0]
    return pl.pallas_call(
        _router_kernel, grid=(N // tm,),
        in_specs=[pl.BlockSpec((tm, D), lambda i: (i, 0)), pl.BlockSpec((1, D), lambda i: (0, 0)),
                  pl.BlockSpec((E, D), lambda i: (0, 0))],
        out_specs=[pl.BlockSpec((E, tm), lambda i: (0, i)), pl.BlockSpec((tm * SUBLANES, LANES), lambda i: (i, 0))],
        out_shape=[jax.ShapeDtypeStruct((E, N), F32), jax.ShapeDtypeStruct((N * SUBLANES, LANES), I32)],
        compiler_params=_cparams(("parallel",), 32), name="moe_router",
    )(x, g.reshape(1, D), wr_t)


def _topk_kernel(aff_ref, pos_ref, off_ref, cnt_ref, *, cap):
    bits = pltpu.bitcast(aff_ref[...], I32)
    nb = bits.shape[0]

    def count(pred):
        return jnp.sum(jnp.sum(jnp.where(pred, 1.0, 0.0), axis=1, keepdims=True), axis=0, keepdims=True)

    def bisect(i, lo):
        cand = lo | lax.shift_left(jnp.int32(1), jnp.int32(30) - i)
        return jnp.where(count(bits >= cand) >= cap, cand, lo)

    thr = lax.fori_loop(0, 31, bisect, jnp.zeros((1, 1), I32))

    r128 = lax.broadcasted_iota(I32, (LANES, LANES), 0)
    c128 = lax.broadcasted_iota(I32, (LANES, LANES), 1)
    upper = jnp.where(r128 <= c128, 1.0, 0.0).astype(BF16)
    rn = lax.broadcasted_iota(I32, (nb, nb), 0)
    cn = lax.broadcasted_iota(I32, (nb, nb), 1)
    strict_lower = jnp.where(cn < rn, 1.0, 0.0).astype(BF16)

    def running_count(m):
        incl = jnp.dot(m.astype(BF16), upper, preferred_element_type=F32)
        tot = jnp.broadcast_to(incl[:, LANES - 1:LANES], (nb, LANES)).astype(BF16)
        return incl, jnp.dot(strict_lower, tot, preferred_element_type=F32)

    gt = jnp.where(bits > thr, 1.0, 0.0)
    eq = jnp.where(bits == thr, 1.0, 0.0)
    need = cap - count(bits > thr)
    eq_incl, eq_off = running_count(eq)
    sel = gt + eq * jnp.where(eq_incl - eq + eq_off < need, 1.0, 0.0)
    incl, off = running_count(sel)
    cnt = incl + off
    pos_ref[...] = jnp.where(sel > 0.0, cnt - 1.0, -1.0).astype(I32)
    off_ref[...] = off.astype(I32)
    cnt_ref[...] = cnt


def topk_slots(aff3, cap):
    E, NB, _ = aff3.shape
    spec = pl.BlockSpec((None, NB, LANES), lambda e: (e, 0, 0))
    return pl.pallas_call(
        functools.partial(_topk_kernel, cap=cap), grid=(E,), in_specs=[spec], out_specs=[spec, spec, spec],
        out_shape=[jax.ShapeDtypeStruct((E, NB, LANES), I32), jax.ShapeDtypeStruct((E, NB, LANES), I32),
                   jax.ShapeDtypeStruct((E, NB, LANES), F32)],
        compiler_params=_cparams(("parallel",), 32), name="moe_topk",
    )(aff3)


def _compact_kernel(blo_ref, bhi_ref, cnt_ref, idx_ref, *, n_chunk):
    e = pl.program_id(0)
    slot = lax.broadcasted_iota(I32, (LANES, LANES), 0)

    def chunk(c, carry):
        lo, hi = blo_ref[e, c], bhi_ref[e, c]
        sigma = (slot + c * LANES).astype(F32)

        def blk(b, acc):
            return acc + jnp.where(cnt_ref[b] <= sigma, 1.0, 0.0)

        acc = lax.fori_loop(lo, hi + 1, blk, jnp.zeros((LANES, LANES), F32))
        col = jnp.sum(acc, axis=1, keepdims=True) + (lo * LANES).astype(F32)
        idx_ref[c] = jnp.broadcast_to(col, (LANES, LANES)).T[0:1, :].astype(I32)
        return carry

    lax.fori_loop(0, n_chunk, chunk, 0)


def compact_slots(cnt3, cap):
    E, NB, _ = cnt3.shape
    n_chunk = cap // LANES
    starts = jnp.arange(n_chunk, dtype=F32) * LANES
    blo = jnp.sum(cnt3[:, :, LANES - 1, None] <= starts[None, None, :], axis=1).astype(I32)
    bhi = jnp.sum(cnt3[:, :, 0, None] <= starts[None, None, :] + (LANES - 1), axis=1).astype(I32) - 1
    grid_spec = pltpu.PrefetchScalarGridSpec(
        num_scalar_prefetch=2, grid=(E,),
        in_specs=[pl.BlockSpec((None, NB, 1, LANES), lambda e, *_: (e, 0, 0, 0))],
        out_specs=pl.BlockSpec((None, n_chunk, 1, LANES), lambda e, *_: (e, 0, 0, 0)))
    idx = pl.pallas_call(
        functools.partial(_compact_kernel, n_chunk=n_chunk), grid_spec=grid_spec,
        out_shape=jax.ShapeDtypeStruct((E, n_chunk, 1, LANES), I32),
        compiler_params=_cparams(("arbitrary",), 32), name="moe_compact",
    )(blo, bhi, cnt3.reshape(E, NB, 1, LANES))
    return idx.reshape(E, cap)


def _ffn_kernel(idx0_ref, idx1_ref, idx2_ref, ext_hbm, wg_ref, wu_ref, wd_ref, o_ref, rows_a, rows_b, sem,
                *, steps_per_expert):
    t, n_steps = pl.program_id(0), pl.num_programs(0)
    tm, d = rows_a.shape[0] // SUBLANES, o_ref.shape[1]

    def row_copy(ids_ref, rows_ref, s, r):
        src = ext_hbm.at[pl.ds(pl.multiple_of(ids_ref[0, 0, r], SUBLANES), SUBLANES)]
        return pltpu.make_async_copy(src, rows_ref.at[pl.ds(r * SUBLANES, SUBLANES)], sem.at[s])

    def start_rows(ids_ref, rows_ref, s):
        for r in range(tm):
            row_copy(ids_ref, rows_ref, s, r).start(priority=r % 2)

    def wait_rows(rows_ref, s):
        pltpu.make_async_copy(ext_hbm.at[pl.ds(0, tm * SUBLANES)], rows_ref, sem.at[s]).wait()

    def ffn(rows_ref, out_rows):
        plane = lambda s: rows_ref[pl.ds(s, tm, stride=SUBLANES), :]
        n_word_rows = d // 2 // LANES
        words = jnp.concatenate([plane(s) for s in range(n_word_rows)], axis=1)
        xn = jnp.concatenate([pltpu.bitcast(lax.shift_left(words, jnp.int32(16)), F32),
                              pltpu.bitcast(words & jnp.int32(-65536), F32)], axis=1).astype(BF16)
        lane = lax.broadcasted_iota(I32, (tm, LANES), 1)
        route_gate = jnp.sum(jnp.where(lane == t // steps_per_expert, pltpu.bitcast(plane(n_word_rows), F32), 0.0),
                             axis=1, keepdims=True)
        gate = jnp.dot(xn, wg_ref[...], preferred_element_type=F32)
        up = jnp.dot(xn, wu_ref[...], preferred_element_type=F32)
        hid = (gate * _sigmoid(gate) * up).astype(BF16)
        o_ref[out_rows, :] = (jnp.dot(hid, wd_ref[...], preferred_element_type=F32) * route_gate).astype(o_ref.dtype)

    @pl.when(t == 0)
    def _():
        def first(r, carry):
            row_copy(idx0_ref, rows_a, 0, r).start()
            return carry
        lax.fori_loop(0, tm, first, 0)

    wait_rows(rows_a, 0)
    start_rows(idx1_ref, rows_b, 1)
    ffn(rows_a, slice(0, tm))
    wait_rows(rows_b, 1)
    start_rows(idx2_ref, rows_a, 0)
    ffn(rows_b, slice(tm, 2 * tm))

    @pl.when(t == n_steps - 1)
    def _():
        wait_rows(rows_a, 0)


def expert_ffn(idx, ext, wg, wu, wd, tm):
    E, cap = idx.shape
    D, FF = wg.shape[1], wg.shape[2]
    steps_per_expert = cap // (2 * tm)
    n_tiles = E * cap // tm
    idx3 = (idx * SUBLANES).reshape(n_tiles, 1, tm)
    smem = lambda imap: pl.BlockSpec((1, 1, tm), imap, memory_space=pltpu.SMEM)
    wspec = lambda shape: pl.BlockSpec((None,) + shape, lambda t: (t // steps_per_expert, 0, 0))
    ye = pl.pallas_call(
        functools.partial(_ffn_kernel, steps_per_expert=steps_per_expert), grid=(n_tiles // 2,),
        in_specs=[smem(lambda t: (2 * t, 0, 0)), smem(lambda t: (2 * t + 1, 0, 0)),
                  smem(lambda t: (jnp.minimum(2 * t + 2, n_tiles - 1), 0, 0)),
                  pl.BlockSpec(memory_space=pl.ANY),
                  wspec((D, FF)), wspec((D, FF)), wspec((FF, D))],
        out_specs=pl.BlockSpec((2 * tm, D), lambda t: (t, 0)),
        out_shape=jax.ShapeDtypeStruct((E * cap, D), BF16),
        scratch_shapes=[pltpu.VMEM((tm * SUBLANES, LANES), I32), pltpu.VMEM((tm * SUBLANES, LANES), I32),
                        pltpu.SemaphoreType.DMA((2,))],
        compiler_params=_cparams(("arbitrary",), 56), name="moe_ffn",
    )(idx3, idx3, idx3, ext, wg, wu, wd)
    return ye.reshape(E, cap, D)


COMBINE_TOKENS = 128


COMBINE_ALIGN = 16
COMBINE_SMALL = 48


def _combine_kernel(s0_small_ref, s0_full_ref, needs_full_ref, x_ref, pos_ref, ye_hbm, o_ref,
                    small_ref, full_ref, sem_small, sem_full):
    t, n_tiles = pl.program_id(0), pl.num_programs(0)
    n_exp = pos_ref.shape[1]
    small, full = small_ref.shape[1] // n_exp, full_ref.shape[1]
    par = t % 2

    def small_window(tile, e, p):
        s0 = pl.multiple_of(s0_small_ref[e, tile], COMBINE_ALIGN)
        return pltpu.make_async_copy(ye_hbm.at[e, pl.ds(s0, small)], small_ref.at[p, pl.ds(e * small, small)],
                                     sem_small.at[p, e])

    def full_window(e):
        s0 = pl.multiple_of(s0_full_ref[e, t], COMBINE_ALIGN)
        return pltpu.make_async_copy(ye_hbm.at[e, pl.ds(s0, full)], full_ref.at[e], sem_full.at[e])

    def scatter_sum(s0_ref, win, rows_of):
        lane = lax.broadcasted_iota(I32, (x_ref.shape[0], win), 1)
        onehots = [jnp.where(pos_ref[:, e:e + 1] == lane + s0_ref[e, t], 1.0, 0.0).astype(BF16)
                   for e in range(n_exp)]
        parts = [jnp.dot(onehots[e], rows_of(e), preferred_element_type=F32) for e in range(n_exp)]
        while len(parts) > 1:
            parts = [a + b for a, b in zip(parts[0::2], parts[1::2])]
        return x_ref[...] + parts[0]

    @pl.when(t == 0)
    def _():
        for e in range(n_exp):
            small_window(0, e, 0).start()

    @pl.when(t + 1 < n_tiles)
    def _():
        for e in range(n_exp):
            small_window(t + 1, e, 1 - par).start(priority=e % 2)

    for e in range(n_exp):
        small_window(t, e, par).wait()

    @pl.when(needs_full_ref[t] == 0)
    def _():
        width = n_exp * small
        col = lax.broadcasted_iota(I32, (1, width), 1)
        owner = col // small
        target = col - owner * small
        wanted = jnp.full((x_ref.shape[0], width), -2, I32)
        for e in range(n_exp):
            target = target + jnp.where(owner == e, s0_small_ref[e, t], 0)
            wanted = jnp.where(owner == e, pos_ref[:, e:e + 1], wanted)
        onehot = jnp.where(wanted == target, 1.0, 0.0).astype(BF16)
        o_ref[...] = x_ref[...] + jnp.dot(onehot, small_ref[par], preferred_element_type=F32)

    @pl.when(needs_full_ref[t] != 0)
    def _():
        for e in range(n_exp):
            full_window(e).start()
        for e in range(n_exp):
            full_window(e).wait()
        o_ref[...] = scatter_sum(s0_full_ref, full, lambda e: full_ref[e])


def moe_combine(x, pos_t, off, ye, tt):
    N, D = x.shape
    E, cap, _ = ye.shape
    full = tt + COMBINE_ALIGN
    small = min(COMBINE_SMALL, full)
    aligned = (off // COMBINE_ALIGN) * COMBINE_ALIGN
    s0_small = jnp.clip(aligned, 0, cap - small).astype(I32)
    s0_full = jnp.clip(aligned, 0, cap - full).astype(I32)
    end = jnp.concatenate([off[:, 1:], jnp.full((E, 1), cap, off.dtype)], axis=1)
    needs_full = jnp.any(end > s0_small + small, axis=0).astype(I32)
    grid_spec = pltpu.PrefetchScalarGridSpec(
        num_scalar_prefetch=3, grid=(N // tt,),
        in_specs=[pl.BlockSpec((tt, D), lambda t, *_: (t, 0)), pl.BlockSpec((tt, E), lambda t, *_: (t, 0)),
                  pl.BlockSpec(memory_space=pl.ANY)],
        out_specs=pl.BlockSpec((tt, D), lambda t, *_: (t, 0)),
        scratch_shapes=[pltpu.VMEM((2, E * small, D), BF16), pltpu.VMEM((E, full, D), BF16),
                        pltpu.SemaphoreType.DMA((2, E)), pltpu.SemaphoreType.DMA((E,))])
    return pl.pallas_call(
        _combine_kernel, grid_spec=grid_spec,
        out_shape=jax.ShapeDtypeStruct((N, D), F32),
        compiler_params=_cparams(("arbitrary",), 40), name="moe_combine",
    )(s0_small, s0_full, needs_full, x, pos_t, ye)


def expert_choice_moe(x, g, wr_t, wg, wu, wd):
    N, D = x.shape
    E = wr_t.shape[0]
    cap = CAPACITY * N // E
    nb = N // LANES
    aff, ext = router(x, g, wr_t, 512)
    pos3, off3, cnt3 = topk_slots(aff.reshape(E, nb, LANES), cap)
    idx = compact_slots(cnt3, cap)
    ye = expert_ffn(idx, ext, wg, wu, wd, min(512, cap // 2))
    return moe_combine(x, pos3.reshape(E, N).T, off3[:, ::COMBINE_TOKENS // LANES, 0], ye, COMBINE_TOKENS)


def _rope_tables(seq):
    pos = jnp.arange(seq, dtype=F32)
    inv = 1.0 / (ROPE_THETA ** (jnp.arange(0, MLA_ROPE, 2, dtype=F32) / MLA_ROPE))
    ang = pos[:, None] * inv[None, :]
    cos, sin = jnp.cos(ang), jnp.sin(ang)
    z32, z64 = jnp.zeros_like(cos), jnp.zeros((seq, LANES - MLA_ROPE), F32)
    return (jnp.concatenate([cos, cos, z64], axis=1), jnp.concatenate([-sin, z32, z64], axis=1),
            jnp.concatenate([z32, sin, z64], axis=1))


def _pad_cols(a, width):
    return jnp.pad(a, [(0, 0)] * (a.ndim - 1) + [(0, width - a.shape[-1])])


def even_layer(x, batch, seq, g_mix, w_in, w_out, lb, hg_gain, qn_g, kvn_g, w_uq, w_ukv, q_gain, k_gain):
    T, D = x.shape
    hgw = 5 * HG_WIDTH
    w_hg = w_in[:, :hgw].astype(BF16)
    w_mla = _pad_cols(w_in[:, hgw:], 6 * LANES).astype(BF16)
    proj_hg, proj_mla = norm_matmul(x, g_mix, [w_hg, w_mla], [F32, F32], 256)

    proj3 = proj_hg.reshape(batch, seq, hgw)
    rows = 512
    o_f = hgrn_direction(proj3, lb[0:1], False, rows)
    o_hg = hgrn_direction(proj3, lb[1:2], True, rows, o_fwd=o_f, gain=hg_gain)

    wq = _pad_cols(w_uq.reshape(MLA_Q_RANK, MLA_HEADS, MLA_QK), MLA_QK_PAD).reshape(MLA_Q_RANK, -1).astype(BF16)
    wkv = w_ukv.astype(BF16)
    qg = _pad_cols(q_gain.reshape(1, MLA_QK), MLA_QK_PAD)
    kg = _pad_cols(k_gain.reshape(1, MLA_QK), MLA_QK_PAD)
    q, k, v = mla_prep(proj_mla, qn_g.reshape(1, -1), kvn_g.reshape(1, -1), wq, wkv, qg, kg,
                       _rope_tables(seq), seq, 512)
    hq = MLA_HEADS * MLA_QK_PAD
    tq, tk = min(seq, 2048), min(seq, 2048)
    o_mla = attention(q.reshape(batch, seq, hq), k.reshape(batch, seq, hq),
                      v.reshape(batch, seq, hq), tq, tk)

    w_o = w_out.astype(BF16)
    return matmul_resid(x, [o_hg.reshape(T, HG_WIDTH), o_mla.reshape(T, MLA_HEADS * MLA_V)],
                        [w_o[:HG_WIDTH], w_o[HG_WIDTH:]], 512)


def odd_layer(x, batch, seq, g_mix, w_in, conv_w, conv_b, wa, ba, wx, bx, lam, w_out):
    T, D = x.shape
    (proj,) = norm_matmul(x, g_mix, [w_in.astype(BF16)], [F32], 256)
    proj3 = proj.reshape(batch, seq, 2 * D)
    sp = jax.nn.softplus(-lam.astype(F32))
    cb = conv_b.reshape(1, D)
    ts = 128
    h_f = lru_direction(proj3, conv_w, cb, wa[0].astype(BF16), ba[0:1], wx[0].astype(BF16), bx[0:1], sp[0:1],
                        False, ts)
    y = lru_direction(proj3, conv_w, cb, wa[1].astype(BF16), ba[1:2], wx[1].astype(BF16), bx[1:2], sp[1:2],
                      True, ts, h_fwd=h_f)
    return matmul_resid(x, [y.reshape(T, D)], [w_out.astype(BF16)], 512)


def _trunk(x3, norm_mix, norm_ffn, ev_w_in, ev_w_out, hg_lb_logits, hg_out_gain, mla_q_norm, mla_kv_norm,
           mla_w_uq, mla_w_ukv, mla_q_gain, mla_k_gain, od_w_in, od_conv_w, od_conv_b, rg_w_a, rg_b_a,
           rg_w_x, rg_b_x, rg_lambda, od_w_out, moe_router, moe_w_gate, moe_w_up, moe_w_down):
    batch, seq, D = x3.shape
    x = x3.reshape(batch * seq, D)
    lb_all = jnp.cumsum(jax.nn.softmax(hg_lb_logits.astype(F32), axis=1), axis=1)
    depth = norm_mix.shape[0]
    for layer in range(depth):
        j = layer // 2
        if layer % 2 == 0:
            x = even_layer(x, batch, seq, norm_mix[layer], ev_w_in[j], ev_w_out[j], lb_all[:, j],
                           hg_out_gain[j], mla_q_norm[j], mla_kv_norm[j], mla_w_uq[j], mla_w_ukv[j],
                           mla_q_gain[j], mla_k_gain[j])
        else:
            x = odd_layer(x, batch, seq, norm_mix[layer], od_w_in[j], od_conv_w[j], od_conv_b[j], rg_w_a[j],
                          rg_b_a[j], rg_w_x[j], rg_b_x[j], rg_lambda[j], od_w_out[j])
        x = expert_choice_moe(x, norm_ffn[layer], moe_router[layer].T, moe_w_gate[layer].astype(BF16),
                              moe_w_up[layer].astype(BF16), moe_w_down[layer].astype(BF16))
    return x.reshape(batch, seq, D)


def kernel(x_prompt, x_sample, norm_mix, norm_ffn, ev_w_in, ev_w_out, hg_lb_logits, hg_out_gain, mla_q_norm,
           mla_kv_norm, mla_w_uq, mla_w_ukv, mla_q_gain, mla_k_gain, od_w_in, od_conv_w, od_conv_b, rg_w_a,
           rg_b_a, rg_w_x, rg_b_x, rg_lambda, od_w_out, moe_router, moe_w_gate, moe_w_up, moe_w_down):
    params = (norm_mix, norm_ffn, ev_w_in, ev_w_out, hg_lb_logits, hg_out_gain, mla_q_norm, mla_kv_norm,
              mla_w_uq, mla_w_ukv, mla_q_gain, mla_k_gain, od_w_in, od_conv_w, od_conv_b, rg_w_a, rg_b_a,
              rg_w_x, rg_b_x, rg_lambda, od_w_out, moe_router, moe_w_gate, moe_w_up, moe_w_down)
    return (_trunk(x_prompt, *params), _trunk(x_sample, *params))
```

```python
import functools

import jax
import jax.numpy as jnp
from jax import lax
from jax.experimental import pallas as pl
from jax.experimental.pallas import tpu as pltpu

F32 = jnp.float32
BF16 = jnp.bfloat16
I32 = jnp.int32
EPS = 1e-6
HIGHEST = lax.Precision.HIGHEST
NT_DIMS = (((1,), (1,)), ((), ()))
TN_DIMS = (((0,), (0,)), ((), ()))

D_MODEL = 1024
HG_HEADS = 4
HG_DK = 128
HG_WIDTH = 512
HG_CHUNK = 64
HG_SUB = 8
HG_EXP_CLAMP = 80.0
MLA_HEADS = 4
MLA_Q_RANK = 384
MLA_KV_RANK = 256
MLA_NOPE = 128
MLA_ROPE = 64
MLA_V = 128
MLA_QK = MLA_NOPE + MLA_ROPE
MLA_QK_PAD = 256
ROPE_THETA = 10000.0
LRU_BLOCKS = 4
LRU_BW = 256
LRU_C = 8.0
N_EXPERTS = 16
CAPACITY = 2
LANES = 128
LOG2_E = 1.4426950408889634
MIB = 1024 * 1024


def _cparams(semantics, vmem_mib):
    return pltpu.CompilerParams(dimension_semantics=semantics, vmem_limit_bytes=vmem_mib * MIB)


def _sigmoid(x):
    return 1.0 / (1.0 + jnp.exp(-x))


def _rms_scale(x, width):
    return lax.rsqrt(jnp.sum(x * x, axis=-1, keepdims=True) * (1.0 / width) + EPS)


def _norm_matmul_kernel(x_ref, g_ref, *refs, n_out):
    x = x_ref[...]
    h = (x * _rms_scale(x, x.shape[-1]) * g_ref[...]).astype(BF16)
    for w_ref, o_ref in zip(refs[:n_out], refs[n_out:]):
        o_ref[...] = jnp.dot(h, w_ref[...], preferred_element_type=F32).astype(o_ref.dtype)


def norm_matmul(x, g, ws, out_dtypes, tm):
    T, D = x.shape
    n = len(ws)
    in_specs = [pl.BlockSpec((tm, D), lambda i: (i, 0)), pl.BlockSpec((1, D), lambda i: (0, 0))]
    in_specs += [pl.BlockSpec(w.shape, lambda i: (0, 0)) for w in ws]
    out_specs = [pl.BlockSpec((tm, w.shape[1]), lambda i: (i, 0)) for w in ws]
    out_shape = [jax.ShapeDtypeStruct((T, w.shape[1]), dt) for w, dt in zip(ws, out_dtypes)]
    return pl.pallas_call(
        functools.partial(_norm_matmul_kernel, n_out=n),
        grid=(T // tm,), in_specs=in_specs, out_specs=out_specs, out_shape=out_shape,
        compiler_params=_cparams(("parallel",), 56), name="norm_matmul",
    )(x, g.reshape(1, D), *ws)


def _matmul_resid_kernel(r_ref, *refs, n_in):
    acc = r_ref[...]
    for a_ref, w_ref in zip(refs[:n_in], refs[n_in:2 * n_in]):
        acc = acc + jnp.dot(a_ref[...], w_ref[...], preferred_element_type=F32)
    refs[2 * n_in][...] = acc


def matmul_resid(resid, acts, ws, tm):
    T, D = resid.shape
    n = len(acts)
    in_specs = [pl.BlockSpec((tm, D), lambda i: (i, 0))]
    in_specs += [pl.BlockSpec((tm, a.shape[1]), lambda i: (i, 0)) for a in acts]
    in_specs += [pl.BlockSpec(w.shape, lambda i: (0, 0)) for w in ws]
    return pl.pallas_call(
        functools.partial(_matmul_resid_kernel, n_in=n),
        grid=(T // tm,), in_specs=in_specs, out_specs=pl.BlockSpec((tm, D), lambda i: (i, 0)),
        out_shape=jax.ShapeDtypeStruct((T, D), F32),
        compiler_params=_cparams(("parallel",), 40), name="matmul_resid",
    )(resid, *acts, *ws)


def _hgrn_kernel(*refs, reverse, final, n_chunk):
    if final:
        q_ref, v_ref, fz_ref, lb_ref, g_ref, of_ref, gain_ref, o_ref, st_ref = refs
    else:
        q_ref, v_ref, fz_ref, lb_ref, o_ref, st_ref = refs
    C, SB = HG_CHUNK, HG_SUB

    @pl.when(pl.program_id(1) == 0)
    def _():
        st_ref[...] = jnp.zeros_like(st_ref)

    row = lax.broadcasted_iota(I32, (C, C), 0)
    col = lax.broadcasted_iota(I32, (C, C), 1)
    blk_of_row = row // SB
    if reverse:
        tri = col >= row
        ref_of_row = blk_of_row * SB
        causal = col >= row
    else:
        tri = col <= row
        ref_of_row = blk_of_row * SB + (SB - 1)
        causal = col <= row
    ref_sum = (col >= ref_of_row) if reverse else (col <= ref_of_row)
    cum_mat = jnp.concatenate([jnp.where(tri, 1.0, 0.0), jnp.where(ref_sum, 1.0, 0.0)], axis=0)
    cum_mat3 = jnp.concatenate([cum_mat, cum_mat, cum_mat], axis=1).astype(BF16)
    row_blk = lax.broadcasted_iota(I32, (C, HG_WIDTH), 0) // SB

    heads = [slice(h * HG_DK, (h + 1) * HG_DK) for h in range(HG_HEADS)]
    order = [((n_chunk - 1 - ci) if reverse else ci) for ci in range(n_chunk)]
    lb = lb_ref[...]

    per_chunk = []
    for c in order:
        rows = slice(c * C, (c + 1) * C)
        q = q_ref[rows, :] * (HG_DK ** -0.5)
        f = lb + (1.0 - lb) * _sigmoid(fz_ref[rows, :])
        logf = jnp.log(f)
        k = 1.0 - f
        hi = logf.astype(BF16)
        r1 = logf - hi.astype(F32)
        mid = r1.astype(BF16)
        lo = (r1 - mid.astype(F32)).astype(BF16)
        bb = jnp.dot(cum_mat3, jnp.concatenate([hi, mid, lo], axis=0), preferred_element_type=F32)
        b, b_blk = bb[:C], bb[C:]
        kt = k * jnp.exp(b_blk - b)
        q_parts, k_parts = [], []
        for j in range(C // SB):
            jr = j * SB if reverse else j * SB + SB - 1
            q_parts.append((q * jnp.exp(jnp.minimum(b - b[jr:jr + 1, :], HG_EXP_CLAMP))).astype(BF16))
            k_parts.append(jnp.where(row_blk == j, kt, 0.0).astype(BF16))
        b_end = b[0:1, :] if reverse else b[C - 1:C, :]
        vb = v_ref[rows, :].astype(BF16)
        intra = []
        for cs in heads:
            scores = lax.dot_general(jnp.concatenate([p[:, cs] for p in q_parts], axis=1),
                                     jnp.concatenate([p[:, cs] for p in k_parts], axis=1),
                                     NT_DIMS, preferred_element_type=F32)
            scores = jnp.where(causal, scores, 0.0).astype(BF16)
            intra.append(jnp.dot(scores, vb[:, cs], preferred_element_type=F32))
        per_chunk.append((rows, intra, (q * jnp.exp(b)).astype(BF16), (k * jnp.exp(b_end - b)).astype(BF16),
                          jnp.exp(b_end), vb))

    states = [st_ref[h] for h in range(HG_HEADS)]
    for rows, intra, q_dec, k_hat, decay_end, vb in per_chunk:
        for h, cs in enumerate(heads):
            o = intra[h] + lax.dot_general(q_dec[:, cs], states[h].astype(BF16), NT_DIMS,
                                           preferred_element_type=F32)
            states[h] = states[h] * decay_end[:, cs] + lax.dot_general(
                vb[:, cs], k_hat[:, cs], TN_DIMS, preferred_element_type=F32)
            if final:
                o = o + of_ref[rows, cs]
                y = o * _rms_scale(o, HG_DK) * gain_ref[...]
                g = g_ref[rows, cs]
                o_ref[rows, cs] = (y * (g * _sigmoid(g))).astype(o_ref.dtype)
            else:
                o_ref[rows, cs] = o
    for h in range(HG_HEADS):
        st_ref[h] = states[h]


def hgrn_direction(proj, lb, reverse, rows, o_fwd=None, gain=None):
    B, S, _ = proj.shape
    W = HG_WIDTH
    nblk = S // rows
    final = o_fwd is not None

    def rowmap(colblk):
        if reverse:
            return lambda b, c: (b, nblk - 1 - c, colblk)
        return lambda b, c: (b, c, colblk)

    blk = (None, rows, W)
    in_specs = [pl.BlockSpec(blk, rowmap(0)), pl.BlockSpec(blk, rowmap(1)),
                pl.BlockSpec(blk, rowmap(4 if reverse else 3)), pl.BlockSpec((1, W), lambda b, c: (0, 0))]
    args = [proj, proj, proj, lb]
    if final:
        in_specs += [pl.BlockSpec(blk, rowmap(2)), pl.BlockSpec(blk, rowmap(0)),
                     pl.BlockSpec((1, HG_DK), lambda b, c: (0, 0))]
        args += [proj, o_fwd, gain.reshape(1, HG_DK)]
    return pl.pallas_call(
        functools.partial(_hgrn_kernel, reverse=reverse, final=final, n_chunk=rows // HG_CHUNK),
        grid=(B, nblk), in_specs=in_specs, out_specs=pl.BlockSpec(blk, rowmap(0)),
        out_shape=jax.ShapeDtypeStruct((B, S, W), BF16 if final else F32),
        scratch_shapes=[pltpu.VMEM((HG_HEADS, HG_DK, HG_DK), F32)],
        compiler_params=_cparams(("parallel", "arbitrary"), 40),
        name="hgrn_bwd" if reverse else "hgrn_fwd",
    )(*args)


def _rope(x, cos_t, sin_a, sin_b):
    return x * cos_t + pltpu.roll(x, LANES - MLA_ROPE // 2, 1) * sin_a + pltpu.roll(x, MLA_ROPE // 2, 1) * sin_b


def _mla_prep_kernel(c_ref, qn_ref, kvn_ref, wq_ref, wkv_ref, qg_ref, kg_ref, cos_ref, sa_ref, sb_ref,
                     q_ref, k_ref, v_ref):
    c = c_ref[...]
    cq = c[:, :MLA_Q_RANK]
    ckv = c[:, MLA_Q_RANK:MLA_Q_RANK + MLA_KV_RANK]
    kpe = c[:, MLA_Q_RANK + MLA_KV_RANK:]
    qn = (cq * _rms_scale(cq, MLA_Q_RANK) * qn_ref[...]).astype(BF16)
    kvn = (ckv * _rms_scale(ckv, MLA_KV_RANK) * kvn_ref[...]).astype(BF16)
    qf = jnp.dot(qn, wq_ref[...], preferred_element_type=F32)
    kvf = jnp.dot(kvn, wkv_ref[...], preferred_element_type=F32)
    cos_t, sin_a, sin_b = cos_ref[...], sa_ref[...], sb_ref[...]
    qg, kg = qg_ref[...], kg_ref[...]
    sm_scale = (MLA_QK ** -0.5) * LOG2_E
    kpe_ss = jnp.sum(kpe * kpe, axis=-1, keepdims=True)
    for h in range(MLA_HEADS):
        lo = h * MLA_QK_PAD
        q_nope, q_rope = qf[:, lo:lo + LANES], qf[:, lo + LANES:lo + 2 * LANES]
        qs = lax.rsqrt((jnp.sum(q_nope * q_nope, axis=-1, keepdims=True)
                        + jnp.sum(q_rope * q_rope, axis=-1, keepdims=True)) * (1.0 / MLA_QK) + EPS)
        q_ref[:, lo:lo + LANES] = (q_nope * qs * qg[:, :LANES] * sm_scale).astype(BF16)
        q_ref[:, lo + LANES:lo + 2 * LANES] = (
            _rope(q_rope * qs * qg[:, LANES:], cos_t, sin_a, sin_b) * sm_scale).astype(BF16)
        k_nope = kvf[:, lo:lo + LANES]
        ks = lax.rsqrt((jnp.sum(k_nope * k_nope, axis=-1, keepdims=True) + kpe_ss) * (1.0 / MLA_QK) + EPS)
        k_ref[:, lo:lo + LANES] = (k_nope * ks * kg[:, :LANES]).astype(BF16)
        k_ref[:, lo + LANES:lo + 2 * LANES] = _rope(kpe * ks * kg[:, LANES:], cos_t, sin_a, sin_b).astype(BF16)
        v_ref[:, lo:lo + LANES] = kvf[:, lo + LANES:lo + 2 * LANES].astype(BF16)
        v_ref[:, lo + LANES:lo + 2 * LANES] = jnp.ones((c.shape[0], LANES), BF16)


def mla_prep(c, qn_g, kvn_g, wq, wkv, qg, kg, tables, seq, tm):
    T, CW = c.shape
    nseq = seq // tm
    full = lambda a: pl.BlockSpec(a.shape, lambda i: (0, 0))
    tab = pl.BlockSpec((tm, LANES), lambda i: (i % nseq, 0))
    HW = MLA_HEADS * MLA_QK_PAD
    return pl.pallas_call(
        _mla_prep_kernel, grid=(T // tm,),
        in_specs=[pl.BlockSpec((tm, CW), lambda i: (i, 0)), full(qn_g), full(kvn_g), full(wq), full(wkv),
                  full(qg), full(kg), tab, tab, tab],
        out_specs=[pl.BlockSpec((tm, HW), lambda i: (i, 0))] * 3,
        out_shape=[jax.ShapeDtypeStruct((T, HW), BF16)] * 3,
        compiler_params=_cparams(("parallel",), 40), name="mla_prep",
    )(c, qn_g, kvn_g, wq, wkv, qg, kg, *tables)


ATTN_PARTS = 8


def _attn_kernel(q_ref, k_ref, v_ref, o_ref, m_ref, l_ref, acc_ref):
    ki = pl.program_id(3)

    @pl.when(ki == 0)
    def _():
        m_ref[...] = jnp.full_like(m_ref, -jnp.inf)
        l_ref[...] = jnp.zeros_like(l_ref)
        acc_ref[...] = jnp.zeros_like(acc_ref)

    k, v = k_ref[...], v_ref[...]
    part = q_ref.shape[0] // ATTN_PARTS
    rows = [slice(i * part, (i + 1) * part) for i in range(ATTN_PARTS)]
    scores = lambda r: lax.dot_general(q_ref[r, :], k, NT_DIMS, preferred_element_type=F32)
    s_next = scores(rows[0])
    for i, r in enumerate(rows):
        s = s_next
        if i + 1 < ATTN_PARTS:
            s_next = scores(rows[i + 1])
        m_prev = m_ref[r, :]
        m_new = jnp.maximum(m_prev, jnp.max(s, axis=-1, keepdims=True))
        alpha = jnp.exp2(m_prev - m_new)
        p = jnp.exp2(s - m_new)
        l_ref[r, :] = alpha * l_ref[r, :] + jnp.sum(p, axis=-1, keepdims=True)
        acc_ref[r, :] = alpha * acc_ref[r, :] + jnp.dot(p.astype(BF16), v, preferred_element_type=F32)
        m_ref[r, :] = m_new

    @pl.when(ki == pl.num_programs(3) - 1)
    def _():
        o_ref[...] = (acc_ref[...] / l_ref[...]).astype(o_ref.dtype)


def attention(q, k, v, tq, tk):
    B, S, _ = q.shape
    return pl.pallas_call(
        _attn_kernel, grid=(B, MLA_HEADS, S // tq, S // tk),
        in_specs=[pl.BlockSpec((None, tq, MLA_QK_PAD), lambda b, h, i, j: (b, i, h)),
                  pl.BlockSpec((None, tk, MLA_QK_PAD), lambda b, h, i, j: (b, j, h)),
                  pl.BlockSpec((None, tk, MLA_V), lambda b, h, i, j: (b, j, 2 * h))],
        out_specs=pl.BlockSpec((None, tq, MLA_V), lambda b, h, i, j: (b, i, h)),
        out_shape=jax.ShapeDtypeStruct((B, S, MLA_HEADS * MLA_V), BF16),
        scratch_shapes=[pltpu.VMEM((tq, 1), F32), pltpu.VMEM((tq, 1), F32), pltpu.VMEM((tq, MLA_V), F32)],
        compiler_params=_cparams(("parallel", "parallel", "parallel", "arbitrary"), 48), name="mla_attention",
    )(q, k, v)


SUBLANES = 8


def _scan_rows(a, u, h_in, reverse):
    n_group = a.shape[0] // SUBLANES
    sub = lax.broadcasted_iota(I32, (SUBLANES, a.shape[1]), 0)
    scanned = []
    for g in range(n_group):
        rows = slice(g * SUBLANES, (g + 1) * SUBLANES)
        ag, ug = a[rows], u[rows]
        k = 1
        while k < SUBLANES:
            shift, valid = (SUBLANES - k, sub < SUBLANES - k) if reverse else (k, sub >= k)
            a_s, u_s = pltpu.roll(ag, shift, 0), pltpu.roll(ug, shift, 0)
            ug = ug + ag * jnp.where(valid, u_s, 0.0)
            ag = ag * jnp.where(valid, a_s, 1.0)
            k *= 2
        scanned.append((ag, ug))
    outs = [None] * n_group
    h = h_in
    for g in (range(n_group - 1, -1, -1) if reverse else range(n_group)):
        ag, ug = scanned[g]
        outs[g] = ug + ag * h
        h = outs[g][0:1] if reverse else outs[g][SUBLANES - 1:SUBLANES]
    return jnp.concatenate(outs, axis=0), h


def _lru_kernel(*refs, reverse, final):
    if final:
        (x_ref, prev_ref, next_ref, cw_ref, cb_ref, wa_ref, ba_ref, wx_ref, bx_ref, sp_ref,
         gate_ref, hf_ref, o_ref, carry_ref) = refs
    else:
        (x_ref, prev_ref, next_ref, cw_ref, cb_ref, wa_ref, ba_ref, wx_ref, bx_ref, sp_ref,
         o_ref, carry_ref) = refs
    step, nstep = pl.program_id(1), pl.num_programs(1)
    blk = (nstep - 1 - step) if reverse else step

    @pl.when(step == 0)
    def _():
        carry_ref[...] = jnp.zeros_like(carry_ref)

    x = x_ref[...]
    ts = x.shape[0]
    rowi = lax.broadcasted_iota(I32, x.shape, 0)
    has_prev = jnp.where(blk > 0, 1.0, 0.0)
    has_next = jnp.where(blk < nstep - 1, 1.0, 0.0)
    p6, p7, n0 = prev_ref[6:7, :] * has_prev, prev_ref[7:8, :] * has_prev, next_ref[0:1, :] * has_next
    xm1 = jnp.where(rowi == 0, p7, pltpu.roll(x, 1, 0))
    xm2 = jnp.where(rowi == 0, p6, jnp.where(rowi == 1, p7, pltpu.roll(x, 2, 0)))
    xp1 = jnp.where(rowi == ts - 1, n0, pltpu.roll(x, ts - 1, 0))
    cw = cw_ref[...]
    xc = xm2 * cw[0:1, :] + xm1 * cw[1:2, :] + x * cw[2:3, :] + xp1 * cw[3:4, :] + cb_ref[...]
    xcb = xc.astype(BF16)

    def block_diag(w_ref, b_ref):
        outs = [jnp.dot(xcb[:, n * LRU_BW:(n + 1) * LRU_BW], w_ref[n], preferred_element_type=F32)
                for n in range(LRU_BLOCKS)]
        return jnp.concatenate(outs, axis=1) + b_ref[...]

    r = _sigmoid(block_diag(wa_ref, ba_ref))
    i = _sigmoid(block_diag(wx_ref, bx_ref))
    log_a = (-LRU_C) * r * sp_ref[...]
    a = jnp.exp(log_a)
    u = jnp.sqrt(1.0 - a * a) * (i * xc)
    h, carry_ref[...] = _scan_rows(a, u, carry_ref[...], reverse)
    if final:
        g = gate_ref[...]
        gelu = 0.5 * g * (1.0 + jnp.tanh(0.7978845608028654 * (g + 0.044715 * (g * g * g))))
        o_ref[...] = ((h + hf_ref[...]) * gelu).astype(o_ref.dtype)
    else:
        o_ref[...] = h


def lru_direction(proj, cw, cb, wa, ba, wx, bx, sp, reverse, ts, h_fwd=None):
    B, S, W2 = proj.shape
    W = W2 // 2
    nblk = S // ts
    final = h_fwd is not None
    sub = ts // 8

    def pos(c):
        return (nblk - 1 - c) if reverse else c

    tile = (None, ts, W)
    halo = (None, 8, W)
    vec = lambda r: pl.BlockSpec((r, W), lambda b, c: (0, 0))
    wspec = pl.BlockSpec((LRU_BLOCKS, LRU_BW, LRU_BW), lambda b, c: (0, 0, 0))
    in_specs = [pl.BlockSpec(tile, lambda b, c: (b, pos(c), 1)),
                pl.BlockSpec(halo, lambda b, c: (b, jnp.maximum(pos(c) * sub - 1, 0), 1)),
                pl.BlockSpec(halo, lambda b, c: (b, jnp.minimum((pos(c) + 1) * sub, S // 8 - 1), 1)),
                vec(4), vec(1), wspec, vec(1), wspec, vec(1), vec(1)]
    args = [proj, proj, proj, cw, cb, wa, ba, wx, bx, sp]
    if final:
        in_specs += [pl.BlockSpec(tile, lambda b, c: (b, pos(c), 0)), pl.BlockSpec(tile, lambda b, c: (b, pos(c), 0))]
        args += [proj, h_fwd]
    return pl.pallas_call(
        functools.partial(_lru_kernel, reverse=reverse, final=final),
        grid=(B, nblk), in_specs=in_specs, out_specs=pl.BlockSpec(tile, lambda b, c: (b, pos(c), 0)),
        out_shape=jax.ShapeDtypeStruct((B, S, W), BF16 if final else F32),
        scratch_shapes=[pltpu.VMEM((1, W), F32)],
        compiler_params=_cparams(("parallel", "arbitrary"), 48),
        name="lru_bwd" if reverse else "lru_fwd",
    )(*args)


def _router_kernel(x_ref, g_ref, wr_ref, aff_ref, ext_ref):
    x = x_ref[...]
    d = x.shape[1]
    xn = x * _rms_scale(x, d) * g_ref[...]
    wr = wr_ref[...]
    x_hi, w_hi = xn.astype(BF16), wr.astype(BF16)
    x_lo, w_lo = (xn - x_hi.astype(F32)).astype(BF16), (wr - w_hi.astype(F32)).astype(BF16)
    logits = lax.dot_general(jnp.concatenate([w_hi, w_hi, w_lo], axis=1), jnp.concatenate([x_hi, x_lo, x_hi], axis=1),
                             NT_DIMS, preferred_element_type=F32)
    e = jnp.exp(logits - jnp.max(logits, axis=0, keepdims=True))
    aff = e / jnp.sum(e, axis=0, keepdims=True)
    aff_ref[...] = aff
    half = d // 2
    tm = x.shape[0]
    as_bits = lambda v: pltpu.bitcast(v.astype(BF16).astype(F32), I32)
    words = lax.shift_right_logical(as_bits(xn[:, :half]), jnp.int32(16)) | as_bits(xn[:, half:])
    aff_rows = jnp.concatenate([aff, jnp.zeros((LANES - aff.shape[0], aff.shape[1]), F32)], axis=0).T
    n_word_rows = half // LANES
    for s in range(SUBLANES):
        if s < n_word_rows:
            piece = words[:, s * LANES:(s + 1) * LANES]
        elif s == n_word_rows:
            piece = pltpu.bitcast(aff_rows, I32)
        else:
            piece = jnp.zeros((tm, LANES), I32)
        ext_ref[pl.ds(s, tm, stride=SUBLANES), :] = piece


def router(x, g, wr_t, tm):
    N, D = x.shape
    E = wr_t.shape[0]
    return pl.pallas_call(
        _router_kernel, grid=(N // tm,),
        in_specs=[pl.BlockSpec((tm, D), lambda i: (i, 0)), pl.BlockSpec((1, D), lambda i: (0, 0)),
                  pl.BlockSpec((E, D), lambda i: (0, 0))],
        out_specs=[pl.BlockSpec((E, tm), lambda i: (0, i)), pl.BlockSpec((tm * SUBLANES, LANES), lambda i: (i, 0))],
        out_shape=[jax.ShapeDtypeStruct((E, N), F32), jax.ShapeDtypeStruct((N * SUBLANES, LANES), I32)],
        compiler_params=_cparams(("parallel",), 32), name="moe_router",
    )(x, g.reshape(1, D), wr_t)


def _topk_kernel(aff_ref, pos_ref, off_ref, cnt_ref, *, cap):
    bits = pltpu.bitcast(aff_ref[...], I32)
    nb = bits.shape[0]

    def count(pred):
        return jnp.sum(jnp.sum(jnp.where(pred, 1.0, 0.0), axis=1, keepdims=True), axis=0, keepdims=True)

    def bisect(i, lo):
        cand = lo | lax.shift_left(jnp.int32(1), jnp.int32(30) - i)
        return jnp.where(count(bits >= cand) >= cap, cand, lo)

    thr = lax.fori_loop(0, 31, bisect, jnp.zeros((1, 1), I32))

    r128 = lax.broadcasted_iota(I32, (LANES, LANES), 0)
    c128 = lax.broadcasted_iota(I32, (LANES, LANES), 1)
    upper = jnp.where(r128 <= c128, 1.0, 0.0).astype(BF16)
    rn = lax.broadcasted_iota(I32, (nb, nb), 0)
    cn = lax.broadcasted_iota(I32, (nb, nb), 1)
    strict_lower = jnp.where(cn < rn, 1.0, 0.0).astype(BF16)

    def running_count(m):
        incl = jnp.dot(m.astype(BF16), upper, preferred_element_type=F32)
        tot = jnp.broadcast_to(incl[:, LANES - 1:LANES], (nb, LANES)).astype(BF16)
        return incl, jnp.dot(strict_lower, tot, preferred_element_type=F32)

    gt = jnp.where(bits > thr, 1.0, 0.0)
    eq = jnp.where(bits == thr, 1.0, 0.0)
    need = cap - count(bits > thr)
    eq_incl, eq_off = running_count(eq)
    sel = gt + eq * jnp.where(eq_incl - eq + eq_off < need, 1.0, 0.0)
    incl, off = running_count(sel)
    cnt = incl + off
    pos_ref[...] = jnp.where(sel > 0.0, cnt - 1.0, -1.0).astype(I32)
    off_ref[...] = off.astype(I32)
    cnt_ref[...] = cnt


def topk_slots(aff3, cap):
    E, NB, _ = aff3.shape
    spec = pl.BlockSpec((None, NB, LANES), lambda e: (e, 0, 0))
    return pl.pallas_call(
        functools.partial(_topk_kernel, cap=cap), grid=(E,), in_specs=[spec], out_specs=[spec, spec, spec],
        out_shape=[jax.ShapeDtypeStruct((E, NB, LANES), I32), jax.ShapeDtypeStruct((E, NB, LANES), I32),
                   jax.ShapeDtypeStruct((E, NB, LANES), F32)],
        compiler_params=_cparams(("parallel",), 32), name="moe_topk",
    )(aff3)


def _compact_kernel(blo_ref, bhi_ref, cnt_ref, idx_ref, *, n_chunk):
    e = pl.program_id(0)
    slot = lax.broadcasted_iota(I32, (LANES, LANES), 0)

    def chunk(c, carry):
        lo, hi = blo_ref[e, c], bhi_ref[e, c]
        sigma = (slot + c * LANES).astype(F32)

        def blk(b, acc):
            return acc + jnp.where(cnt_ref[b] <= sigma, 1.0, 0.0)

        acc = lax.fori_loop(lo, hi + 1, blk, jnp.zeros((LANES, LANES), F32))
        col = jnp.sum(acc, axis=1, keepdims=True) + (lo * LANES).astype(F32)
        idx_ref[c] = jnp.broadcast_to(col, (LANES, LANES)).T[0:1, :].astype(I32)
        return carry

    lax.fori_loop(0, n_chunk, chunk, 0)


def compact_slots(cnt3, cap):
    E, NB, _ = cnt3.shape
    n_chunk = cap // LANES
    starts = jnp.arange(n_chunk, dtype=F32) * LANES
    blo = jnp.sum(cnt3[:, :, LANES - 1, None] <= starts[None, None, :], axis=1).astype(I32)
    bhi = jnp.sum(cnt3[:, :, 0, None] <= starts[None, None, :] + (LANES - 1), axis=1).astype(I32) - 1
    grid_spec = pltpu.PrefetchScalarGridSpec(
        num_scalar_prefetch=2, grid=(E,),
        in_specs=[pl.BlockSpec((None, NB, 1, LANES), lambda e, *_: (e, 0, 0, 0))],
        out_specs=pl.BlockSpec((None, n_chunk, 1, LANES), lambda e, *_: (e, 0, 0, 0)))
    idx = pl.pallas_call(
        functools.partial(_compact_kernel, n_chunk=n_chunk), grid_spec=grid_spec,
        out_shape=jax.ShapeDtypeStruct((E, n_chunk, 1, LANES), I32),
        compiler_params=_cparams(("arbitrary",), 32), name="moe_compact",
    )(blo, bhi, cnt3.reshape(E, NB, 1, LANES))
    return idx.reshape(E, cap)


def _ffn_kernel(idx0_ref, idx1_ref, idx2_ref, ext_hbm, wg_ref, wu_ref, wd_ref, o_ref, rows_a, rows_b, sem,
                *, steps_per_expert):
    t, n_steps = pl.program_id(0), pl.num_programs(0)
    tm, d = rows_a.shape[0] // SUBLANES, o_ref.shape[1]

    def row_copy(ids_ref, rows_ref, s, r):
        src = ext_hbm.at[pl.ds(pl.multiple_of(ids_ref[0, 0, r], SUBLANES), SUBLANES)]
        return pltpu.make_async_copy(src, rows_ref.at[pl.ds(r * SUBLANES, SUBLANES)], sem.at[s])

    def start_rows(ids_ref, rows_ref, s):
        for r in range(tm):
            row_copy(ids_ref, rows_ref, s, r).start(priority=r % 2)

    def wait_rows(rows_ref, s):
        pltpu.make_async_copy(ext_hbm.at[pl.ds(0, tm * SUBLANES)], rows_ref, sem.at[s]).wait()

    def ffn(rows_ref, out_rows):
        plane = lambda s: rows_ref[pl.ds(s, tm, stride=SUBLANES), :]
        n_word_rows = d // 2 // LANES
        words = jnp.concatenate([plane(s) for s in range(n_word_rows)], axis=1)
        xn = jnp.concatenate([pltpu.bitcast(lax.shift_left(words, jnp.int32(16)), F32),
                              pltpu.bitcast(words & jnp.int32(-65536), F32)], axis=1).astype(BF16)
        lane = lax.broadcasted_iota(I32, (tm, LANES), 1)
        route_gate = jnp.sum(jnp.where(lane == t // steps_per_expert, pltpu.bitcast(plane(n_word_rows), F32), 0.0),
                             axis=1, keepdims=True)
        gate = jnp.dot(xn, wg_ref[...], preferred_element_type=F32)
        up = jnp.dot(xn, wu_ref[...], preferred_element_type=F32)
        hid = (gate * _sigmoid(gate) * up).astype(BF16)
        o_ref[out_rows, :] = (jnp.dot(hid, wd_ref[...], preferred_element_type=F32) * route_gate).astype(o_ref.dtype)

    @pl.when(t == 0)
    def _():
        def first(r, carry):
            row_copy(idx0_ref, rows_a, 0, r).start()
            return carry
        lax.fori_loop(0, tm, first, 0)

    wait_rows(rows_a, 0)
    start_rows(idx1_ref, rows_b, 1)
    ffn(rows_a, slice(0, tm))
    wait_rows(rows_b, 1)
    start_rows(idx2_ref, rows_a, 0)
    ffn(rows_b, slice(tm, 2 * tm))

    @pl.when(t == n_steps - 1)
    def _():
        wait_rows(rows_a, 0)


def expert_ffn(idx, ext, wg, wu, wd, tm):
    E, cap = idx.shape
    D, FF = wg.shape[1], wg.shape[2]
    steps_per_expert = cap // (2 * tm)
    n_tiles = E * cap // tm
    idx3 = (idx * SUBLANES).reshape(n_tiles, 1, tm)
    smem = lambda imap: pl.BlockSpec((1, 1, tm), imap, memory_space=pltpu.SMEM)
    wspec = lambda shape: pl.BlockSpec((None,) + shape, lambda t: (t // steps_per_expert, 0, 0))
    ye = pl.pallas_call(
        functools.partial(_ffn_kernel, steps_per_expert=steps_per_expert), grid=(n_tiles // 2,),
        in_specs=[smem(lambda t: (2 * t, 0, 0)), smem(lambda t: (2 * t + 1, 0, 0)),
                  smem(lambda t: (jnp.minimum(2 * t + 2, n_tiles - 1), 0, 0)),
                  pl.BlockSpec(memory_space=pl.ANY),
                  wspec((D, FF)), wspec((D, FF)), wspec((FF, D))],
        out_specs=pl.BlockSpec((2 * tm, D), lambda t: (t, 0)),
        out_shape=jax.ShapeDtypeStruct((E * cap, D), BF16),
        scratch_shapes=[pltpu.VMEM((tm * SUBLANES, LANES), I32), pltpu.VMEM((tm * SUBLANES, LANES), I32),
                        pltpu.SemaphoreType.DMA((2,))],
        compiler_params=_cparams(("arbitrary",), 56), name="moe_ffn",
    )(idx3, idx3, idx3, ext, wg, wu, wd)
    return ye.reshape(E, cap, D)


COMBINE_TOKENS = 128


COMBINE_ALIGN = 16
COMBINE_SMALL = 48


def _combine_kernel(s0_small_ref, s0_full_ref, needs_full_ref, x_ref, pos_ref, ye_hbm, o_ref,
                    small_ref, full_ref, sem_small, sem_full):
    t, n_tiles = pl.program_id(0), pl.num_programs(0)
    n_exp = pos_ref.shape[1]
    small, full = small_ref.shape[1] // n_exp, full_ref.shape[1]
    par = t % 2

    def small_window(tile, e, p):
        s0 = pl.multiple_of(s0_small_ref[e, tile], COMBINE_ALIGN)
        return pltpu.make_async_copy(ye_hbm.at[e, pl.ds(s0, small)], small_ref.at[p, pl.ds(e * small, small)],
                                     sem_small.at[p, e])

    def full_window(e):
        s0 = pl.multiple_of(s0_full_ref[e, t], COMBINE_ALIGN)
        return pltpu.make_async_copy(ye_hbm.at[e, pl.ds(s0, full)], full_ref.at[e], sem_full.at[e])

    def scatter_sum(s0_ref, win, rows_of):
        lane = lax.broadcasted_iota(I32, (x_ref.shape[0], win), 1)
        onehots = [jnp.where(pos_ref[:, e:e + 1] == lane + s0_ref[e, t], 1.0, 0.0).astype(BF16)
                   for e in range(n_exp)]
        parts = [jnp.dot(onehots[e], rows_of(e), preferred_element_type=F32) for e in range(n_exp)]
        while len(parts) > 1:
            parts = [a + b for a, b in zip(parts[0::2], parts[1::2])]
        return x_ref[...] + parts[0]

    @pl.when(t == 0)
    def _():
        for e in range(n_exp):
            small_window(0, e, 0).start()

    @pl.when(t + 1 < n_tiles)
    def _():
        for e in range(n_exp):
            small_window(t + 1, e, 1 - par).start(priority=e % 2)

    for e in range(n_exp):
        small_window(t, e, par).wait()

    @pl.when(needs_full_ref[t] == 0)
    def _():
        width = n_exp * small
        col = lax.broadcasted_iota(I32, (1, width), 1)
        owner = col // small
        target = col - owner * small
        wanted = jnp.full((x_ref.shape[0], width), -2, I32)
        for e in range(n_exp):
            target = target + jnp.where(owner == e, s0_small_ref[e, t], 0)
            wanted = jnp.where(owner == e, pos_ref[:, e:e + 1], wanted)
        onehot = jnp.where(wanted == target, 1.0, 0.0).astype(BF16)
        o_ref[...] = x_ref[...] + jnp.dot(onehot, small_ref[par], preferred_element_type=F32)

    @pl.when(needs_full_ref[t] != 0)
    def _():
        for e in range(n_exp):
            full_window(e).start()
        for e in range(n_exp):
            full_window(e).wait()
        o_ref[...] = scatter_sum(s0_full_ref, full, lambda e: full_ref[e])


def moe_combine(x, pos_t, off, ye, tt):
    N, D = x.shape
    E, cap, _ = ye.shape
    full = tt + COMBINE_ALIGN
    small = min(COMBINE_SMALL, full)
    aligned = (off // COMBINE_ALIGN) * COMBINE_ALIGN
    s0_small = jnp.clip(aligned, 0, cap - small).astype(I32)
    s0_full = jnp.clip(aligned, 0, cap - full).astype(I32)
    end = jnp.concatenate([off[:, 1:], jnp.full((E, 1), cap, off.dtype)], axis=1)
    needs_full = jnp.any(end > s0_small + small, axis=0).astype(I32)
    grid_spec = pltpu.PrefetchScalarGridSpec(
        num_scalar_prefetch=3, grid=(N // tt,),
        in_specs=[pl.BlockSpec((tt, D), lambda t, *_: (t, 0)), pl.BlockSpec((tt, E), lambda t, *_: (t, 0)),
                  pl.BlockSpec(memory_space=pl.ANY)],
        out_specs=pl.BlockSpec((tt, D), lambda t, *_: (t, 0)),
        scratch_shapes=[pltpu.VMEM((2, E * small, D), BF16), pltpu.VMEM((E, full, D), BF16),
                        pltpu.SemaphoreType.DMA((2, E)), pltpu.SemaphoreType.DMA((E,))])
    return pl.pallas_call(
        _combine_kernel, grid_spec=grid_spec,
        out_shape=jax.ShapeDtypeStruct((N, D), F32),
        compiler_params=_cparams(("arbitrary",), 40), name="moe_combine",
    )(s0_small, s0_full, needs_full, x, pos_t, ye)


def expert_choice_moe(x, g, wr_t, wg, wu, wd):
    N, D = x.shape
    E = wr_t.shape[0]
    cap = CAPACITY * N // E
    nb = N // LANES
    aff, ext = router(x, g, wr_t, 512)
    pos3, off3, cnt3 = topk_slots(aff.reshape(E, nb, LANES), cap)
    idx = compact_slots(cnt3, cap)
    ye = expert_ffn(idx, ext, wg, wu, wd, min(512, cap // 2))
    return moe_combine(x, pos3.reshape(E, N).T, off3[:, ::COMBINE_TOKENS // LANES, 0], ye, COMBINE_TOKENS)


def _rope_tables(seq):
    pos = jnp.arange(seq, dtype=F32)
    inv = 1.0 / (ROPE_THETA ** (jnp.arange(0, MLA_ROPE, 2, dtype=F32) / MLA_ROPE))
    ang = pos[:, None] * inv[None, :]
    cos, sin = jnp.cos(ang), jnp.sin(ang)
    z32, z64 = jnp.zeros_like(cos), jnp.zeros((seq, LANES - MLA_ROPE), F32)
    return (jnp.concatenate([cos, cos, z64], axis=1), jnp.concatenate([-sin, z32, z64], axis=1),
            jnp.concatenate([z32, sin, z64], axis=1))


def _pad_cols(a, width):
    return jnp.pad(a, [(0, 0)] * (a.ndim - 1) + [(0, width - a.shape[-1])])


def even_layer(x, batch, seq, g_mix, w_in, w_out, lb, hg_gain, qn_g, kvn_g, w_uq, w_ukv, q_gain, k_gain):
    T, D = x.shape
    hgw = 5 * HG_WIDTH
    w_hg = w_in[:, :hgw].astype(BF16)
    w_mla = _pad_cols(w_in[:, hgw:], 6 * LANES).astype(BF16)
    proj_hg, proj_mla = norm_matmul(x, g_mix, [w_hg, w_mla], [F32, F32], 256)

    proj3 = proj_hg.reshape(batch, seq, hgw)
    rows = 512
    o_f = hgrn_direction(proj3, lb[0:1], False, rows)
    o_hg = hgrn_direction(proj3, lb[1:2], True, rows, o_fwd=o_f, gain=hg_gain)

    wq = _pad_cols(w_uq.reshape(MLA_Q_RANK, MLA_HEADS, MLA_QK), MLA_QK_PAD).reshape(MLA_Q_RANK, -1).astype(BF16)
    wkv = w_ukv.astype(BF16)
    qg = _pad_cols(q_gain.reshape(1, MLA_QK), MLA_QK_PAD)
    kg = _pad_cols(k_gain.reshape(1, MLA_QK), MLA_QK_PAD)
    q, k, v = mla_prep(proj_mla, qn_g.reshape(1, -1), kvn_g.reshape(1, -1), wq, wkv, qg, kg,
                       _rope_tables(seq), seq, 512)
    hq = MLA_HEADS * MLA_QK_PAD
    tq, tk = min(seq, 2048), min(seq, 2048)
    o_mla = attention(q.reshape(batch, seq, hq), k.reshape(batch, seq, hq),
                      v.reshape(batch, seq, hq), tq, tk)

    w_o = w_out.astype(BF16)
    return matmul_resid(x, [o_hg.reshape(T, HG_WIDTH), o_mla.reshape(T, MLA_HEADS * MLA_V)],
                        [w_o[:HG_WIDTH], w_o[HG_WIDTH:]], 512)


def odd_layer(x, batch, seq, g_mix, w_in, conv_w, conv_b, wa, ba, wx, bx, lam, w_out):
    T, D = x.shape
    (proj,) = norm_matmul(x, g_mix, [w_in.astype(BF16)], [F32], 256)
    proj3 = proj.reshape(batch, seq, 2 * D)
    sp = jax.nn.softplus(-lam.astype(F32))
    cb = conv_b.reshape(1, D)
    ts = 128
    h_f = lru_direction(proj3, conv_w, cb, wa[0].astype(BF16), ba[0:1], wx[0].astype(BF16), bx[0:1], sp[0:1],
                        False, ts)
    y = lru_direction(proj3, conv_w, cb, wa[1].astype(BF16), ba[1:2], wx[1].astype(BF16), bx[1:2], sp[1:2],
                      True, ts, h_fwd=h_f)
    return matmul_resid(x, [y.reshape(T, D)], [w_out.astype(BF16)], 512)


def _trunk(x3, norm_mix, norm_ffn, ev_w_in, ev_w_out, hg_lb_logits, hg_out_gain, mla_q_norm, mla_kv_norm,
           mla_w_uq, mla_w_ukv, mla_q_gain, mla_k_gain, od_w_in, od_conv_w, od_conv_b, rg_w_a, rg_b_a,
           rg_w_x, rg_b_x, rg_lambda, od_w_out, moe_router, moe_w_gate, moe_w_up, moe_w_down):
    batch, seq, D = x3.shape
    x = x3.reshape(batch * seq, D)
    lb_all = jnp.cumsum(jax.nn.softmax(hg_lb_logits.astype(F32), axis=1), axis=1)
    depth = norm_mix.shape[0]
    for layer in range(depth):
        j = layer // 2
        if layer % 2 == 0:
            x = even_layer(x, batch, seq, norm_mix[layer], ev_w_in[j], ev_w_out[j], lb_all[:, j],
                           hg_out_gain[j], mla_q_norm[j], mla_kv_norm[j], mla_w_uq[j], mla_w_ukv[j],
                           mla_q_gain[j], mla_k_gain[j])
        else:
            x = odd_layer(x, batch, seq, norm_mix[layer], od_w_in[j], od_conv_w[j], od_conv_b[j], rg_w_a[j],
                          rg_b_a[j], rg_w_x[j], rg_b_x[j], rg_lambda[j], od_w_out[j])
        x = expert_choice_moe(x, norm_ffn[layer], moe_router[layer].T, moe_w_gate[layer].astype(BF16),
                              moe_w_up[layer].astype(BF16), moe_w_down[layer].astype(BF16))
    return x.reshape(batch, seq, D)


def kernel(x_prompt, x_sample, norm_mix, norm_ffn, ev_w_in, ev_w_out, hg_lb_logits, hg_out_gain, mla_q_norm,
           mla_kv_norm, mla_w_uq, mla_w_ukv, mla_q_gain, mla_k_gain, od_w_in, od_conv_w, od_conv_b, rg_w_a,
           rg_b_a, rg_w_x, rg_b_x, rg_lambda, od_w_out, moe_router, moe_w_gate, moe_w_up, moe_w_down):
    params = (norm_mix, norm_ffn, ev_w_in, ev_w_out, hg_lb_logits, hg_out_gain, mla_q_norm, mla_kv_norm,
              mla_w_uq, mla_w_ukv, mla_q_gain, mla_k_gain, od_w_in, od_conv_w, od_conv_b, rg_w_a, rg_b_a,
              rg_w_x, rg_b_x, rg_lambda, od_w_out, moe_router, moe_w_gate, moe_w_up, moe_w_down)
    return (_trunk(x_prompt, *params), _trunk(x_sample, *params))
```
